```python
import math
import jax, jax.numpy as jnp
from jax import lax
import numpy as np

D_MODEL = 2048
BATCH = 2
SEQ = 16384
DEPTH = 2

N_HEADS_MLA = 16
Q_LORA = 512
KV_LORA = 512
QK_NOPE = 128
QK_ROPE = 64
V_DIM = 128
ROPE_THETA = 10000.0
Q_BLOCK = 128
SGU_CHUNK = 128
SGU_GROUPS = 16
SGU_WIDTH = D_MODEL
SGU_GROUP_DIM = SGU_WIDTH // SGU_GROUPS
POOL_WINDOWS = (2, 4, 8, 16)
POOL_GROUPS = 4
POOL_WIDTH = D_MODEL
POOL_GROUP_DIM = POOL_WIDTH // POOL_GROUPS
N_EXPERTS = 32
TOP_K = 4
D_EXPERT = D_MODEL // 2
SWIGLU_LIMIT = 7.0
SWIGLU_ALPHA = 1.702
MOE_BLOCK = 128
N_BRANCHES = 3
EPS = 1e-6
MLA_IN = Q_LORA + KV_LORA + QK_ROPE
IN_COLS = MLA_IN + 2 * SGU_WIDTH + POOL_WIDTH + N_BRANCHES * D_MODEL

kernel_name = 'hybrid_mla_sgu_pool_moe_adaln'


def rmsnorm(x, g):
    xf = x.astype(jnp.float32)
    y = xf * lax.rsqrt(jnp.mean(xf * xf, axis=-1, keepdims=True) + EPS)
    return (y * g.astype(jnp.float32)).astype(x.dtype)


def layernorm(x, g):
    xf = x.astype(jnp.float32)
    mu = jnp.mean(xf, axis=-1, keepdims=True)
    var = jnp.mean(jnp.square(xf - mu), axis=-1, keepdims=True)
    return ((xf - mu) * lax.rsqrt(var + EPS) * g.astype(jnp.float32)).astype(x.dtype)


def rope_tables(S):
    pos = jnp.arange(S, dtype=jnp.float32)
    inv_freq = ROPE_THETA ** (-jnp.arange(0, QK_ROPE, 2, dtype=jnp.float32) / QK_ROPE)
    ang = pos[:, None] * inv_freq[None, :]
    return jnp.cos(ang), jnp.sin(ang)


def apply_rope(x, cos, sin):
    half = x.shape[-1] // 2
    x1 = x[..., :half].astype(jnp.float32)
    x2 = x[..., half:].astype(jnp.float32)
    return jnp.concatenate([x1 * cos - x2 * sin, x2 * cos + x1 * sin], axis=-1).astype(x.dtype)


def mla_branch(lat_q, lat_kv, k_rope_raw, cos, sin, q_norm, kv_norm, w_uq, w_ukv):
    B, S, _ = lat_q.shape
    c_q = rmsnorm(lat_q, q_norm)
    q = jnp.einsum('bsr,rhd->bshd', c_q, w_uq)
    q_nope = q[..., :QK_NOPE]
    q_pe = apply_rope(q[..., QK_NOPE:], cos[None, :, None, :], sin[None, :, None, :])
    c_kv = rmsnorm(lat_kv, kv_norm)
    kv = jnp.einsum('bsr,rhd->bshd', c_kv, w_ukv)
    k_nope = kv[..., :QK_NOPE]
    v = kv[..., QK_NOPE:]
    k_pe = apply_rope(k_rope_raw, cos[None], sin[None])
    scale = (QK_NOPE + QK_ROPE) ** -0.5
    n_blocks = S // Q_BLOCK
    qn_blk = q_nope.reshape(B, n_blocks, Q_BLOCK, N_HEADS_MLA, QK_NOPE).transpose(1, 0, 2, 3, 4)
    qp_blk = q_pe.reshape(B, n_blocks, Q_BLOCK, N_HEADS_MLA, QK_ROPE).transpose(1, 0, 2, 3, 4)
    key_idx = jnp.arange(S)

    def attend(args):
        qn, qp, blk = args
        s = (jnp.einsum('bqhd,bkhd->bhqk', qn, k_nope)
             + jnp.einsum('bqhd,bkd->bhqk', qp, k_pe)).astype(jnp.float32) * scale
        q_idx = blk * Q_BLOCK + jnp.arange(Q_BLOCK)
        causal = key_idx[None, :] <= q_idx[:, None]
        s = jnp.where(causal[None, None], s, jnp.float32(-1e30))
        p = jax.nn.softmax(s, axis=-1).astype(v.dtype)
        return jnp.einsum('bhqk,bkhd->bqhd', p, v)

    out = lax.map(attend, (qn_blk, qp_blk, jnp.arange(n_blocks)))
    return out.transpose(1, 0, 2, 3, 4).reshape(B, S, N_HEADS_MLA * V_DIM)


def sgu_branch(z, sgu_norm, w_s, b_s):
    B, S, _ = z.shape
    z = jax.nn.gelu(z)
    u, v = z[..., :SGU_WIDTH], z[..., SGU_WIDTH:]
    v = layernorm(v, sgu_norm)
    n = S // SGU_CHUNK
    vb = v.reshape(B, n, SGU_CHUNK, SGU_GROUPS, SGU_GROUP_DIM)
    tri = jnp.tril(jnp.ones((SGU_CHUNK, SGU_CHUNK), dtype=w_s.dtype))
    w = w_s * tri[None]
    mixed = jnp.einsum('gts,bnsgc->bntgc', w, vb) + b_s.T[None, None, :, :, None]
    return u * mixed.reshape(B, S, SGU_WIDTH)


def pool_branch(p, w_pool, pool_scale):
    B, S, _ = p.shape
    pf = p.astype(jnp.float32)
    count = jnp.arange(1, S + 1, dtype=jnp.float32)
    outs = []
    for gi, win in enumerate(POOL_WINDOWS):
        xg = pf[..., gi * POOL_GROUP_DIM:(gi + 1) * POOL_GROUP_DIM]
        cs = jnp.cumsum(xg, axis=1)
        lagged = jnp.pad(cs, ((0, 0), (win, 0), (0, 0)))[:, :S]
        cnt = jnp.minimum(count, jnp.float32(win))[None, :, None]
        d = ((cs - lagged) / cnt - xg).astype(p.dtype)
        outs.append(jnp.einsum('bsc,cd->bsd', d, w_pool[gi]))
    return jnp.concatenate(outs, axis=-1) * pool_scale


def mixer_block(h, cos, sin, w_in, q_norm, kv_norm, w_uq, w_ukv, sgu_norm, w_s, b_s,
                w_pool, pool_scale, w_o):
    B, S, D = h.shape
    proj = jnp.einsum('bsd,de->bse', h, w_in)
    bounds = [Q_LORA, Q_LORA + KV_LORA, MLA_IN, MLA_IN + 2 * SGU_WIDTH,
              MLA_IN + 2 * SGU_WIDTH + POOL_WIDTH]
    lat_q, lat_kv, k_rope_raw, z_sgu, p_pool, gates = jnp.split(proj, bounds, axis=-1)
    y_a = mla_branch(lat_q, lat_kv, k_rope_raw, cos, sin, q_norm, kv_norm, w_uq, w_ukv)
    y_b = sgu_branch(z_sgu, sgu_norm, w_s, b_s)
    y_c = pool_branch(p_pool, w_pool, pool_scale)
    g = jax.nn.sigmoid(gates.reshape(B, S, N_BRANCHES, D))
    merged = g[:, :, 0] * y_a + g[:, :, 1] * y_b + g[:, :, 2] * y_c
    return jnp.einsum('bsd,de->bse', merged, w_o)


def moe_ffn(h, w_router, b_router, w_gate, b_gate, w_up, b_up, w_down, b_down):
    B, S, D = h.shape
    T = B * S
    xt = h.reshape(T, D)
    logits = (xt @ w_router + b_router).astype(jnp.float32)
    top_vals, top_idx = lax.top_k(logits, TOP_K)
    weights = jax.nn.softmax(top_vals, axis=-1)
    M = T * TOP_K
    flat_e = top_idx.reshape(M)
    order = jnp.argsort(flat_e)
    sorted_e = flat_e[order]
    tok = order // TOP_K
    counts = jnp.bincount(flat_e, length=N_EXPERTS)
    padded = ((counts + MOE_BLOCK - 1) // MOE_BLOCK) * MOE_BLOCK
    start = jnp.cumsum(counts) - counts
    pad_end = jnp.cumsum(padded)
    pad_start = pad_end - padded
    dest = pad_start[sorted_e] + (jnp.arange(M) - start[sorted_e])
    n_blocks = M // MOE_BLOCK + N_EXPERTS
    rows = n_blocks * MOE_BLOCK
    buf = jnp.zeros((rows, D), h.dtype).at[dest].set(xt[tok])
    block_e = jnp.minimum(
        jnp.searchsorted(pad_end, jnp.arange(n_blocks) * MOE_BLOCK, side='right'), N_EXPERTS - 1)

    def expert_block(args):
        xb, e = args
        gl = xb @ w_gate[e] + b_gate[e]
        li = xb @ w_up[e] + b_up[e]
        gl = jnp.minimum(gl, SWIGLU_LIMIT)
        li = jnp.clip(li, -SWIGLU_LIMIT, SWIGLU_LIMIT)
        act = gl * jax.nn.sigmoid(SWIGLU_ALPHA * gl) * (li + 1)
        return act @ w_down[e] + b_down[e]

    y_buf = lax.map(expert_block, (buf.reshape(n_blocks, MOE_BLOCK, D), block_e)).reshape(rows, D)
    y_sorted = y_buf[dest] * weights.reshape(M)[order][:, None].astype(h.dtype)
    y = jax.ops.segment_sum(y_sorted, tok, num_segments=T)
    return y.reshape(B, S, D)


def setup_inputs(seed: int = 0) -> dict:
    key = jax.random.key(seed)
    ks = jax.random.split(key, 32)
    f32 = jnp.float32
    D, L, E, F = D_MODEL, DEPTH, N_EXPERTS, D_EXPERT

    def nrm(k, shape, s):
        return jax.random.normal(k, shape, f32) * s

    def gain(k, shape):
        return 1.0 + 0.02 * jax.random.normal(k, shape, f32)

    return {
        'x': nrm(ks[0], (BATCH, SEQ, D), 1.0),
        'c': nrm(ks[1], (BATCH, D), 1.0),
        'attn_norm': gain(ks[2], (L, D)),
        'ffn_norm': gain(ks[3], (L, D)),
        'w_ada': nrm(ks[4], (L, D, 6 * D), 0.5 * D ** -0.5),
        'b_ada': nrm(ks[5], (L, 6 * D), 0.02),
        'w_in': nrm(ks[6], (L, D, IN_COLS), D ** -0.5),
        'q_norm': gain(ks[7], (L, Q_LORA)),
        'kv_norm': gain(ks[8], (L, KV_LORA)),
        'w_uq': nrm(ks[9], (L, Q_LORA, N_HEADS_MLA, QK_NOPE + QK_ROPE), Q_LORA ** -0.5),
        'w_ukv': nrm(ks[10], (L, KV_LORA, N_HEADS_MLA, QK_NOPE + V_DIM), KV_LORA ** -0.5),
        'sgu_norm': gain(ks[11], (L, SGU_WIDTH)),
        'w_s': nrm(ks[12], (L, SGU_GROUPS, SGU_CHUNK, SGU_CHUNK), SGU_CHUNK ** -0.5),
        'b_s': gain(ks[13], (L, SGU_GROUPS, SGU_CHUNK)),
        'w_pool': nrm(ks[14], (L, POOL_GROUPS, POOL_GROUP_DIM, POOL_GROUP_DIM), POOL_GROUP_DIM ** -0.5),
        'pool_scale': gain(ks[15], (L, POOL_WIDTH)),
        'w_o': nrm(ks[16], (L, D, D), D ** -0.5),
        'w_router': nrm(ks[17], (L, D, E), D ** -0.5),
        'b_router': nrm(ks[18], (L, E), 0.01),
        'w_gate': nrm(ks[19], (L, E, D, F), D ** -0.5),
        'b_gate': nrm(ks[20], (L, E, F), 0.02),
        'w_up': nrm(ks[21], (L, E, D, F), D ** -0.5),
        'b_up': nrm(ks[22], (L, E, F), 0.02),
        'w_down': nrm(ks[23], (L, E, F, D), F ** -0.5),
        'b_down': nrm(ks[24], (L, E, D), 0.02),
        'final_norm': gain(ks[25], (D,)),
    }


def reference(x, c, attn_norm, ffn_norm, w_ada, b_ada, w_in, q_norm, kv_norm, w_uq, w_ukv,
              sgu_norm, w_s, b_s, w_pool, pool_scale, w_o, w_router, b_router,
              w_gate, b_gate, w_up, b_up, w_down, b_down, final_norm):
    S = x.shape[1]
    cos, sin = rope_tables(S)
    for l in range(DEPTH):
        mod = jnp.einsum('bd,de->be', jax.nn.silu(c), w_ada[l]) + b_ada[l]
        sh1, sc1, g1, sh2, sc2, g2 = jnp.split(mod[:, None, :], 6, axis=-1)
        h = rmsnorm(x, attn_norm[l]) * (1 + sc1) + sh1
        x = x + g1 * mixer_block(h, cos, sin, w_in[l], q_norm[l], kv_norm[l], w_uq[l], w_ukv[l],
                                 sgu_norm[l], w_s[l], b_s[l], w_pool[l], pool_scale[l], w_o[l])
        h = rmsnorm(x, ffn_norm[l]) * (1 + sc2) + sh2
        x = x + g2 * moe_ffn(h, w_router[l], b_router[l], w_gate[l], b_gate[l], w_up[l], b_up[l],
                             w_down[l], b_down[l])
    return rmsnorm(x, final_norm)
```

```python
import functools
import math

import jax
import jax.numpy as jnp
from jax import lax
from jax.experimental import pallas as pl
from jax.experimental.pallas import tpu as pltpu

F32 = jnp.float32
BF16 = jnp.bfloat16
U32 = jnp.uint32
I32 = jnp.int32

N_HEADS = 16
Q_LORA = 512
KV_LORA = 512
QK_NOPE = 128
QK_ROPE = 64
V_DIM = 128
ROPE_THETA = 10000.0
SGU_CHUNK = 128
SGU_GROUPS = 16
POOL_WINDOWS = (2, 4, 8, 16)
POOL_HALO = 16
N_EXPERTS = 32
TOP_K = 4
SWIGLU_LIMIT = 7.0
SWIGLU_ALPHA = 1.702
EPS = 1e-6
LANES = 128
MOE_ROWS = 256
MIB = 1024 * 1024


def _cparams(semantics, vmem_mib):
    return pltpu.CompilerParams(dimension_semantics=semantics, vmem_limit_bytes=vmem_mib * MIB)


def _adaln_kernel(c_ref, w_ref, b_ref, o_ref):
    c = c_ref[...]
    a = c * jax.nn.sigmoid(c)
    o_ref[...] = jnp.dot(a, w_ref[...], preferred_element_type=F32,
                         precision=lax.Precision.HIGHEST) + b_ref[...]


def adaln(c, w_ada, b_ada):
    n_layers, d, n = w_ada.shape
    rows = 8
    cp = jnp.zeros((rows, d), F32).at[:c.shape[0]].set(c)
    tn = 1024
    return pl.pallas_call(
        _adaln_kernel,
        out_shape=jax.ShapeDtypeStruct((n_layers, rows, n), F32),
        grid=(n_layers, n // tn),
        in_specs=[pl.BlockSpec((rows, d), lambda l, j: (0, 0)),
                  pl.BlockSpec((None, d, tn), lambda l, j: (l, 0, j)),
                  pl.BlockSpec((None, 1, tn), lambda l, j: (l, 0, j))],
        out_specs=pl.BlockSpec((None, rows, tn), lambda l, j: (l, 0, j)),
        compiler_params=_cparams(("parallel", "parallel"), 40),
        name="adaln",
    )(cp, w_ada, b_ada.reshape(n_layers, 1, n))


def _modulated_rmsnorm(x, gain, scale, shift):
    y = x * lax.rsqrt(jnp.mean(x * x, axis=-1, keepdims=True) + EPS) * gain
    return y * (1.0 + scale) + shift


def _norm_matmul_kernel(x_ref, g_ref, sc_ref, sh_ref, w_ref, o_ref, h_ref):
    @pl.when(pl.program_id(2) == 0)
    def _():
        h = _modulated_rmsnorm(x_ref[...], g_ref[...], sc_ref[...], sh_ref[...])
        h_ref[...] = h.astype(BF16)

    o_ref[...] = jnp.dot(h_ref[...], w_ref[...], preferred_element_type=F32)


def norm_matmul(x, gain, scale, shift, w, tn):
    b, s, d = x.shape
    n = w.shape[1]
    tm = min(1024, s)
    return pl.pallas_call(
        _norm_matmul_kernel,
        out_shape=jax.ShapeDtypeStruct((b, s, n), F32),
        grid=(b, s // tm, n // tn),
        in_specs=[pl.BlockSpec((None, tm, d), lambda bi, i, j: (bi, i, 0)),
                  pl.BlockSpec((1, d), lambda bi, i, j: (0, 0)),
                  pl.BlockSpec((None, 1, d), lambda bi, i, j: (bi, 0, 0)),
                  pl.BlockSpec((None, 1, d), lambda bi, i, j: (bi, 0, 0)),
                  pl.BlockSpec((d, tn), lambda bi, i, j: (0, j))],
        out_specs=pl.BlockSpec((None, tm, tn), lambda bi, i, j: (bi, i, j)),
        scratch_shapes=[pltpu.VMEM((tm, d), BF16)],
        compiler_params=_cparams(("parallel", "parallel", "arbitrary"), 56),
        name="norm_matmul",
    )(x, gain.reshape(1, d), scale, shift, w)


def _rms(x, gain):
    return x * lax.rsqrt(jnp.mean(x * x, axis=-1, keepdims=True) + EPS) * gain


def _rope_tile(t, cc, ss):
    return t * cc + pltpu.roll(t, LANES // 2, 1) * ss


def _mla_prep_kernel(lat_ref, qn_ref, kvn_ref, cc_ref, ss_ref, wq_ref, wkv_ref,
                     q_ref, k_ref, v_ref, cq_ref, ckv_ref, kpe_ref):
    @pl.when(pl.program_id(2) == 0)
    def _():
        cq_ref[...] = _rms(lat_ref[:, :Q_LORA], qn_ref[...]).astype(BF16)
        ckv_ref[...] = _rms(lat_ref[:, Q_LORA:Q_LORA + KV_LORA], kvn_ref[...]).astype(BF16)
        kp = lat_ref[:, Q_LORA + KV_LORA:]
        kpe_ref[...] = _rope_tile(kp, cc_ref[...], ss_ref[...]).astype(BF16)

    scale = (QK_NOPE + QK_ROPE) ** -0.5
    qres = jnp.dot(cq_ref[...], wq_ref[...], preferred_element_type=F32)
    q_ref[:, :QK_NOPE] = (qres[:, :QK_NOPE] * scale).astype(BF16)
    q_ref[:, QK_NOPE:] = (_rope_tile(qres[:, QK_NOPE:], cc_ref[...], ss_ref[...]) * scale).astype(BF16)
    kvres = jnp.dot(ckv_ref[...], wkv_ref[...], preferred_element_type=F32)
    k_ref[:, :QK_NOPE] = kvres[:, :QK_NOPE].astype(BF16)
    k_ref[:, QK_NOPE:] = kpe_ref[...]
    v_ref[...] = kvres[:, QK_NOPE:].astype(BF16)


def mla_prep(lat, q_norm, kv_norm, cc, ss, wq, wkv):
    b, s, nlat = lat.shape
    tm = min(512, s)
    hd = QK_NOPE + LANES
    return pl.pallas_call(
        _mla_prep_kernel,
        out_shape=(jax.ShapeDtypeStruct((b, N_HEADS, s, hd), BF16),
                   jax.ShapeDtypeStruct((b, N_HEADS, s, hd), BF16),
                   jax.ShapeDtypeStruct((b, N_HEADS, s, V_DIM), BF16)),
        grid=(b, s // tm, N_HEADS),
        in_specs=[pl.BlockSpec((None, tm, nlat), lambda bi, i, h: (bi, i, 0)),
                  pl.BlockSpec((1, Q_LORA), lambda bi, i, h: (0, 0)),
                  pl.BlockSpec((1, KV_LORA), lambda bi, i, h: (0, 0)),
                  pl.BlockSpec((tm, LANES), lambda bi, i, h: (i, 0)),
                  pl.BlockSpec((tm, LANES), lambda bi, i, h: (i, 0)),
                  pl.BlockSpec((None, Q_LORA, hd), lambda bi, i, h: (h, 0, 0)),
                  pl.BlockSpec((None, KV_LORA, QK_NOPE + V_DIM), lambda bi, i, h: (h, 0, 0))],
        out_specs=(pl.BlockSpec((None, None, tm, hd), lambda bi, i, h: (bi, h, i, 0)),
                   pl.BlockSpec((None, None, tm, hd), lambda bi, i, h: (bi, h, i, 0)),
                   pl.BlockSpec((None, None, tm, V_DIM), lambda bi, i, h: (bi, h, i, 0))),
        scratch_shapes=[pltpu.VMEM((tm, Q_LORA), BF16), pltpu.VMEM((tm, KV_LORA), BF16),
                        pltpu.VMEM((tm, LANES), BF16)],
        compiler_params=_cparams(("parallel", "parallel", "arbitrary"), 32),
        name="mla_prep",
    )(lat, q_norm.reshape(1, -1), kv_norm.reshape(1, -1), cc, ss, wq, wkv)


def _flash_kernel(q_ref, k_ref, v_ref, o_ref, m_ref, l_ref, acc_ref):
    qi = pl.program_id(2)
    ki = pl.program_id(3)

    @pl.when(ki == 0)
    def _():
        m_ref[...] = jnp.full_like(m_ref, -jnp.inf)
        l_ref[...] = jnp.zeros_like(l_ref)
        acc_ref[...] = jnp.zeros_like(acc_ref)

    def step(masked):
        s = lax.dot_general(q_ref[...], k_ref[...], (((1,), (1,)), ((), ())),
                            preferred_element_type=F32)
        if masked:
            row = lax.broadcasted_iota(I32, s.shape, 0)
            col = lax.broadcasted_iota(I32, s.shape, 1)
            s = jnp.where(col <= row, s, -1e30)
        m_prev = m_ref[...]
        m_new = jnp.maximum(m_prev, jnp.max(s, axis=-1, keepdims=True))
        alpha = jnp.exp(m_prev - m_new)
        p = jnp.exp(s - m_new)
        l_ref[...] = alpha * l_ref[...] + jnp.sum(p, axis=-1, keepdims=True)
        acc_ref[...] = alpha * acc_ref[...] + jnp.dot(p.astype(BF16), v_ref[...],
                                                      preferred_element_type=F32)
        m_ref[...] = m_new

    @pl.when(ki < qi)
    def _():
        step(False)

    @pl.when(ki == qi)
    def _():
        step(True)
        o_ref[...] = (acc_ref[...] / l_ref[...]).astype(o_ref.dtype)


def flash(q, k, v):
    b, h, s, hd = q.shape
    t = min(512, s)
    kv_idx = lambda bi, hi, qi, ki: (bi, hi, jnp.minimum(ki, qi), 0)
    return pl.pallas_call(
        _flash_kernel,
        out_shape=jax.ShapeDtypeStruct((b, s, h * V_DIM), BF16),
        grid=(b, h, s // t, s // t),
        in_specs=[pl.BlockSpec((None, None, t, hd), lambda bi, hi, qi, ki: (bi, hi, qi, 0)),
                  pl.BlockSpec((None, None, t, hd), kv_idx),
                  pl.BlockSpec((None, None, t, V_DIM), kv_idx)],
        out_specs=pl.BlockSpec((None, t, V_DIM), lambda bi, hi, qi, ki: (bi, qi, hi)),
        scratch_shapes=[pltpu.VMEM((t, 1), F32), pltpu.VMEM((t, 1), F32), pltpu.VMEM((t, V_DIM), F32)],
        compiler_params=_cparams(("parallel", "parallel", "parallel", "arbitrary"), 32),
        name="flash",
    )(q, k, v)


def _gelu_tanh(x):
    return 0.5 * x * (1.0 + jnp.tanh(math.sqrt(2.0 / math.pi) * (x + 0.044715 * (x * x * x))))


def _mixer_merge_kernel(z_ref, p_ref, halo_ref, gt_ref, ya_ref, sn_ref, ws_ref, bs_ref,
                        wp_ref, ps_ref, o_ref, acc_ref):
    tm, d = o_ref.shape
    gd = d // SGU_GROUPS
    tri_r = lax.broadcasted_iota(I32, (SGU_CHUNK, SGU_CHUNK), 0)
    tri_c = lax.broadcasted_iota(I32, (SGU_CHUNK, SGU_CHUNK), 1)
    tril = tri_c <= tri_r

    acc_ref[...] = jax.nn.sigmoid(gt_ref[:, :d]) * ya_ref[...].astype(F32)

    for c in range(tm // SGU_CHUNK):
        rows = pl.ds(c * SGU_CHUNK, SGU_CHUNK)
        v = _gelu_tanh(z_ref[rows, d:])
        mu = jnp.mean(v, axis=-1, keepdims=True)
        vc = v - mu
        var = jnp.mean(vc * vc, axis=-1, keepdims=True)
        vn = (vc * lax.rsqrt(var + EPS) * sn_ref[...]).astype(BF16)
        for g in range(SGU_GROUPS):
            cols = pl.ds(g * gd, gd)
            w = jnp.where(tril, ws_ref[g], jnp.zeros((), BF16))
            mixed = jnp.dot(w, vn[:, g * gd:(g + 1) * gd], preferred_element_type=F32) + bs_ref[:, g:g + 1]
            yb = _gelu_tanh(z_ref[rows, cols]) * mixed
            gate = jax.nn.sigmoid(gt_ref[rows, pl.ds(d + g * gd, gd)])
            acc_ref[rows, cols] = acc_ref[rows, cols] + gate * yb

    pgd = d // len(POOL_WINDOWS)
    first = pl.program_id(1) == 0
    pos = (pl.program_id(1) * tm + lax.broadcasted_iota(I32, (tm, 1), 0) + 1).astype(F32)
    for gi, win in enumerate(POOL_WINDOWS):
        cols = pl.ds(gi * pgd, pgd)
        xg = p_ref[:, cols]
        halo = jnp.where(first, 0.0, halo_ref[:, cols])
        e = jnp.concatenate([halo, xg], axis=0)
        span = 1
        while span < win:
            e = e + pltpu.roll(e, span, 0)
            span *= 2
        wsum = e[POOL_HALO:]
        dlt = wsum / jnp.minimum(pos, float(win)) - xg
        yc = jnp.dot(dlt.astype(BF16), wp_ref[gi], preferred_element_type=F32) * ps_ref[:, cols]
        gate = jax.nn.sigmoid(gt_ref[:, pl.ds(2 * d + gi * pgd, pgd)])
        o_ref[:, cols] = (acc_ref[:, cols] + gate * yc).astype(o_ref.dtype)


def mixer_merge(z, p, gates, ya, sgu_norm, ws, bs_t, wp, pool_scale):
    b, s, d = p.shape
    tm = min(256, s)
    hb = tm // POOL_HALO
    return pl.pallas_call(
        _mixer_merge_kernel,
        out_shape=jax.ShapeDtypeStruct((b, s, d), BF16),
        grid=(b, s // tm),
        in_specs=[pl.BlockSpec((None, tm, 2 * d), lambda bi, i: (bi, i, 0)),
                  pl.BlockSpec((None, tm, d), lambda bi, i: (bi, i, 0)),
                  pl.BlockSpec((None, POOL_HALO, d), lambda bi, i: (bi, jnp.maximum(i * hb - 1, 0), 0)),
                  pl.BlockSpec((None, tm, 3 * d), lambda bi, i: (bi, i, 0)),
                  pl.BlockSpec((None, tm, d), lambda bi, i: (bi, i, 0)),
                  pl.BlockSpec((1, d), lambda bi, i: (0, 0)),
                  pl.BlockSpec(ws.shape, lambda bi, i: (0, 0, 0)),
                  pl.BlockSpec(bs_t.shape, lambda bi, i: (0, 0)),
                  pl.BlockSpec(wp.shape, lambda bi, i: (0, 0, 0)),
                  pl.BlockSpec((1, d), lambda bi, i: (0, 0))],
        out_specs=pl.BlockSpec((None, tm, d), lambda bi, i: (bi, i, 0)),
        scratch_shapes=[pltpu.VMEM((tm, d), F32)],
        compiler_params=_cparams(("parallel", "arbitrary"), 56),
        name="mixer_merge",
    )(z, p, p, gates, ya, sgu_norm.reshape(1, d), ws, bs_t, wp, pool_scale.reshape(1, d))


def _matmul_res_kernel(a_ref, w_ref, x_ref, g_ref, o_ref):
    y = jnp.dot(a_ref[...], w_ref[...], preferred_element_type=F32)
    o_ref[...] = x_ref[...] + g_ref[...] * y


def matmul_res(a, w, x, gate):
    b, s, d = x.shape
    tm = min(512, s)
    return pl.pallas_call(
        _matmul_res_kernel,
        out_shape=jax.ShapeDtypeStruct((b, s, d), F32),
        grid=(b, s // tm),
        in_specs=[pl.BlockSpec((None, tm, d), lambda bi, i: (bi, i, 0)),
                  pl.BlockSpec((d, d), lambda bi, i: (0, 0)),
                  pl.BlockSpec((None, tm, d), lambda bi, i: (bi, i, 0)),
                  pl.BlockSpec((None, 1, d), lambda bi, i: (bi, 0, 0))],
        out_specs=pl.BlockSpec((None, tm, d), lambda bi, i: (bi, i, 0)),
        compiler_params=_cparams(("parallel", "parallel"), 56),
        name="matmul_res",
    )(a, w, x, gate)


def _pack_bf16_pair(a, b):
    pa = lax.bitcast_convert_type(a.astype(BF16).astype(F32), U32)
    pb = lax.bitcast_convert_type(b.astype(BF16).astype(F32), U32)
    return pa | (pb >> 16)


def _unpack_bf16_pair(w):
    hi = lax.bitcast_convert_type(w & jnp.uint32(0xFFFF0000), F32)
    lo = lax.bitcast_convert_type(w << 16, F32)
    return hi.astype(BF16), lo.astype(BF16)


def _router_kernel(x_ref, g_ref, sc_ref, sh_ref, wr_ref, br_ref,
                   hp_ref, e_ref, w_ref, r_ref, cnt_ref, carry_ref):
    tm, d = x_ref.shape
    half = d // 2

    @pl.when((pl.program_id(0) == 0) & (pl.program_id(1) == 0))
    def _():
        carry_ref[...] = jnp.zeros_like(carry_ref)

    h = _modulated_rmsnorm(x_ref[...], g_ref[...], sc_ref[...], sh_ref[...])
    hp_ref[...] = _pack_bf16_pair(h[:, :half], h[:, half:])

    logits = lax.dot_general(wr_ref[...], h, (((1,), (1,)), ((), ())),
                             preferred_element_type=F32, precision=lax.Precision.HIGHEST) + br_ref[...]
    eio = lax.broadcasted_iota(I32, logits.shape, 0).astype(F32)
    onehot = jnp.zeros(logits.shape, F32)
    vals, sels = [], []
    rem = logits
    for k in range(TOP_K):
        m = jnp.max(rem, axis=0, keepdims=True)
        idx = jnp.min(jnp.where(rem == m, eio, float(N_EXPERTS)), axis=0, keepdims=True)
        sel = eio == idx
        rem = jnp.where(sel, -jnp.inf, rem)
        onehot = onehot + jnp.where(sel, 1.0, 0.0)
        vals.append(m)
        sels.append(sel)
        e_ref[k:k + 1, :] = idx.astype(I32)
    ex = [jnp.exp(v - vals[0]) for v in vals]
    den = ex[0] + ex[1] + ex[2] + ex[3]
    for k in range(TOP_K):
        w_ref[k:k + 1, :] = ex[k] / den

    before = lax.broadcasted_iota(I32, (tm, tm), 0) < lax.broadcasted_iota(I32, (tm, tm), 1)
    cum = jnp.dot(onehot.astype(BF16), jnp.where(before, 1.0, 0.0).astype(BF16),
                  preferred_element_type=F32) + carry_ref[...]
    for k in range(TOP_K):
        r_ref[k:k + 1, :] = jnp.sum(jnp.where(sels[k], cum, 0.0), axis=0, keepdims=True).astype(I32)
    carry_ref[...] = carry_ref[...] + jnp.sum(onehot, axis=1, keepdims=True)
    cnt_ref[...] = jnp.broadcast_to(carry_ref[...], cnt_ref.shape).astype(I32)


def router(x, gain, scale, shift, wr_t, b_router):
    b, s, d = x.shape
    tm = min(512, s)
    nt = s // tm
    t = b * s
    tok = lambda bi, i: (0, bi * nt + i)
    return pl.pallas_call(
        _router_kernel,
        out_shape=(jax.ShapeDtypeStruct((b, s, d // 2), U32),
                   jax.ShapeDtypeStruct((TOP_K, t), I32),
                   jax.ShapeDtypeStruct((TOP_K, t), F32),
                   jax.ShapeDtypeStruct((TOP_K, t), I32),
                   jax.ShapeDtypeStruct((N_EXPERTS, LANES), I32)),
        grid=(b, nt),
        in_specs=[pl.BlockSpec((None, tm, d), lambda bi, i: (bi, i, 0)),
                  pl.BlockSpec((1, d), lambda bi, i: (0, 0)),
                  pl.BlockSpec((None, 1, d), lambda bi, i: (bi, 0, 0)),
                  pl.BlockSpec((None, 1, d), lambda bi, i: (bi, 0, 0)),
                  pl.BlockSpec((N_EXPERTS, d), lambda bi, i: (0, 0)),
                  pl.BlockSpec((N_EXPERTS, 1), lambda bi, i: (0, 0))],
        out_specs=(pl.BlockSpec((None, tm, d // 2), lambda bi, i: (bi, i, 0)),
                   pl.BlockSpec((TOP_K, tm), tok),
                   pl.BlockSpec((TOP_K, tm), tok),
                   pl.BlockSpec((TOP_K, tm), tok),
                   pl.BlockSpec((N_EXPERTS, LANES), lambda bi, i: (0, 0))),
        scratch_shapes=[pltpu.VMEM((N_EXPERTS, 1), F32)],
        compiler_params=_cparams(("arbitrary", "arbitrary"), 40),
        name="router",
    )(x, gain.reshape(1, d), scale, shift, wr_t, b_router.reshape(N_EXPERTS, 1))


def _dest_kernel(ps_ref, e_ref, r_ref, o_ref):
    e = e_ref[...]
    base = jnp.zeros(e.shape, I32)
    for j in range(N_EXPERTS):
        base = jnp.where(e == j, ps_ref[j], base)
    o_ref[...] = base + r_ref[...]


def dest_rows(pad_start, e_t, r_t):
    return pl.pallas_call(
        _dest_kernel,
        out_shape=jax.ShapeDtypeStruct(e_t.shape, I32),
        grid_spec=pltpu.PrefetchScalarGridSpec(
            num_scalar_prefetch=1, grid=(1,),
            in_specs=[pl.BlockSpec(e_t.shape, lambda i, ps: (0, 0)),
                      pl.BlockSpec(e_t.shape, lambda i, ps: (0, 0))],
            out_specs=pl.BlockSpec(e_t.shape, lambda i, ps: (0, 0))),
        compiler_params=_cparams(("arbitrary",), 16),
        name="dest",
    )(pad_start, e_t, r_t)


def _dispatch_kernel(dest_ref, h_ref, buf_in_ref, buf_ref, sem):
    del buf_in_ref
    tm = h_ref.shape[0]

    def row_copy(r, d):
        return pltpu.make_async_copy(h_ref.at[pl.ds(r, 1)], buf_ref.at[pl.ds(d, 1)], sem)

    def issue(r, carry):
        for k in range(TOP_K):
            row_copy(r, dest_ref[r * TOP_K + k]).start()
        return carry

    lax.fori_loop(0, tm, issue, 0)

    def drain(j, carry):
        row_copy(0, 0).wait()
        return carry

    lax.fori_loop(0, tm * TOP_K, drain, 0)


def dispatch(hp, dest_flat, n_rows):
    t, w = hp.shape
    tm = min(256, t)
    buf0 = jnp.zeros((n_rows, w), hp.dtype)
    return pl.pallas_call(
        _dispatch_kernel,
        out_shape=jax.ShapeDtypeStruct((n_rows, w), hp.dtype),
        grid=(t // tm,),
        in_specs=[pl.BlockSpec((tm * TOP_K,), lambda i: (i,), memory_space=pltpu.SMEM),
                  pl.BlockSpec((tm, w), lambda i: (i, 0)),
                  pl.BlockSpec(memory_space=pl.ANY)],
        out_specs=pl.BlockSpec(memory_space=pl.ANY),
        scratch_shapes=[pltpu.SemaphoreType.DMA],
        input_output_aliases={2: 0},
        compiler_params=_cparams(("arbitrary",), 16),
        name="dispatch",
    )(dest_flat, hp, buf0)


def _experts_kernel(be_ref, nu_ref, x_ref, wg_ref, bg_ref, wu_ref, bu_ref, wd_ref, bd_ref, o_ref):
    del be_ref

    @pl.when(pl.program_id(0) < nu_ref[0])
    def _():
        hi, lo = _unpack_bf16_pair(x_ref[...])
        x = jnp.concatenate([hi, lo], axis=1)
        gl = jnp.dot(x, wg_ref[...], preferred_element_type=F32) + bg_ref[...]
        li = jnp.dot(x, wu_ref[...], preferred_element_type=F32) + bu_ref[...]
        gl = jnp.minimum(gl, SWIGLU_LIMIT)
        li = jnp.clip(li, -SWIGLU_LIMIT, SWIGLU_LIMIT)
        act = gl * jax.nn.sigmoid(SWIGLU_ALPHA * gl) * (li + 1.0)
        o_ref[...] = jnp.dot(act.astype(BF16), wd_ref[...], preferred_element_type=F32) + bd_ref[...]

    @pl.when(pl.program_id(0) >= nu_ref[0])
    def _():
        o_ref[...] = jnp.zeros_like(o_ref)


def experts(buf, block_e, n_used, wg, bg, wu, bu, wd, bd):
    n_rows, w = buf.shape
    n_blocks = n_rows // MOE_ROWS
    n_e, d, f = wg.shape
    row_idx = lambda i, be, nu: (jnp.minimum(i, nu[0] - 1), 0)
    return pl.pallas_call(
        _experts_kernel,
        out_shape=jax.ShapeDtypeStruct((n_rows, d), F32),
        grid_spec=pltpu.PrefetchScalarGridSpec(
            num_scalar_prefetch=2, grid=(n_blocks,),
            in_specs=[pl.BlockSpec((MOE_ROWS, w), row_idx),
                      pl.BlockSpec((None, d, f), lambda i, be, nu: (be[i], 0, 0)),
                      pl.BlockSpec((None, 1, f), lambda i, be, nu: (be[i], 0, 0)),
                      pl.BlockSpec((None, d, f), lambda i, be, nu: (be[i], 0, 0)),
                      pl.BlockSpec((None, 1, f), lambda i, be, nu: (be[i], 0, 0)),
                      pl.BlockSpec((None, f, d), lambda i, be, nu: (be[i], 0, 0)),
                      pl.BlockSpec((None, 1, d), lambda i, be, nu: (be[i], 0, 0))],
            out_specs=pl.BlockSpec((MOE_ROWS, d), lambda i, be, nu: (i, 0))),
        compiler_params=_cparams(("arbitrary",), 56),
        name="experts",
    )(block_e, n_used, buf, wg, bg.reshape(n_e, 1, f), wu, bu.reshape(n_e, 1, f), wd, bd.reshape(n_e, 1, d))


def _combine_kernel(dest_ref, w_ref, y_ref, x_ref, g_ref, fn_ref, o_ref, rows_ref, sem, *, final):
    tm = x_ref.shape[0]

    def row_copy(d, k, r):
        return pltpu.make_async_copy(y_ref.at[pl.ds(d, 1)], rows_ref.at[k, pl.ds(r, 1)], sem)

    def issue(r, carry):
        for k in range(TOP_K):
            row_copy(dest_ref[r * TOP_K + k], k, r).start()
        return carry

    lax.fori_loop(0, tm, issue, 0)

    def drain(j, carry):
        row_copy(0, 0, 0).wait()
        return carry

    lax.fori_loop(0, tm * TOP_K, drain, 0)

    y = w_ref[:, 0:1] * rows_ref[0]
    for k in range(1, TOP_K):
        y = y + w_ref[:, k:k + 1] * rows_ref[k]
    out = x_ref[...] + g_ref[...] * y
    if final:
        out = _rms(out, fn_ref[...])
    o_ref[...] = out


def combine(dest_flat, w_tok, ybuf, x, gate, final_norm, final):
    b, s, d = x.shape
    tm = min(256, s)
    nt = s // tm
    return pl.pallas_call(
        functools.partial(_combine_kernel, final=final),
        out_shape=jax.ShapeDtypeStruct((b, s, d), F32),
        grid=(b, nt),
        in_specs=[pl.BlockSpec((tm * TOP_K,), lambda bi, i: (bi * nt + i,), memory_space=pltpu.SMEM),
                  pl.BlockSpec((tm, TOP_K), lambda bi, i: (bi * nt + i, 0)),
                  pl.BlockSpec(memory_space=pl.ANY),
                  pl.BlockSpec((None, tm, d), lambda bi, i: (bi, i, 0)),
                  pl.BlockSpec((None, 1, d), lambda bi, i: (bi, 0, 0)),
                  pl.BlockSpec((1, d), lambda bi, i: (0, 0))],
        out_specs=pl.BlockSpec((None, tm, d), lambda bi, i: (bi, i, 0)),
        scratch_shapes=[pltpu.VMEM((TOP_K, tm, d), F32), pltpu.SemaphoreType.DMA],
        compiler_params=_cparams(("arbitrary", "arbitrary"), 40),
        name="combine",
    )(dest_flat, w_tok, ybuf, x, gate, final_norm.reshape(1, d))


def _swap_halves(w):
    half = w.shape[-1] // 2
    return jnp.concatenate([w[..., half:], w[..., :half]], axis=-1)


def _rope_tables(s):
    pos = jnp.arange(s, dtype=F32)
    inv_freq = ROPE_THETA ** (-jnp.arange(0, QK_ROPE, 2, dtype=F32) / QK_ROPE)
    ang = pos[:, None] * inv_freq[None, :]
    cos, sin = jnp.cos(ang), jnp.sin(ang)
    zero = jnp.zeros((s, LANES - QK_ROPE), F32)
    return (jnp.concatenate([cos, cos, zero], axis=1), jnp.concatenate([-sin, sin, zero], axis=1))


def kernel(x, c, attn_norm, ffn_norm, w_ada, b_ada, w_in, q_norm, kv_norm, w_uq, w_ukv, sgu_norm, w_s, b_s,
           w_pool, pool_scale, w_o, w_router, b_router, w_gate, b_gate, w_up, b_up, w_down, b_down, final_norm):
    b, s, d = x.shape
    n_layers = w_ada.shape[0]
    t = b * s
    mla_in = Q_LORA + KV_LORA + QK_ROPE
    mods = adaln(c, w_ada, b_ada)
    cc, ss = _rope_tables(s)
    n_blocks = t * TOP_K // MOE_ROWS + N_EXPERTS
    n_rows = n_blocks * MOE_ROWS

    for l in range(n_layers):
        sh1, sc1, g1, sh2, sc2, g2 = [mods[l, :b, i * d:(i + 1) * d].reshape(b, 1, d) for i in range(6)]

        wl = w_in[l]
        k_rope_cols = wl[:, Q_LORA + KV_LORA:mla_in]
        w_lat = jnp.concatenate([wl[:, :mla_in], _swap_halves(k_rope_cols)], axis=1).astype(BF16)
        w_z = wl[:, mla_in:mla_in + 2 * d].astype(BF16)
        w_p = wl[:, mla_in + 2 * d:mla_in + 3 * d].astype(BF16)
        w_g = wl[:, mla_in + 3 * d:].astype(BF16)
        lat = norm_matmul(x, attn_norm[l], sc1, sh1, w_lat, w_lat.shape[1])
        z = norm_matmul(x, attn_norm[l], sc1, sh1, w_z, 1024)
        p = norm_matmul(x, attn_norm[l], sc1, sh1, w_p, 1024)
        gates = norm_matmul(x, attn_norm[l], sc1, sh1, w_g, 1024)

        wq = w_uq[l]
        wq = jnp.concatenate([wq, _swap_halves(wq[..., QK_NOPE:])], axis=-1)
        wq = wq.transpose(1, 0, 2).astype(BF16)
        wkv = w_ukv[l].transpose(1, 0, 2).astype(BF16)
        q, k, v = mla_prep(lat, q_norm[l], kv_norm[l], cc, ss, wq, wkv)
        ya = flash(q, k, v)

        merged = mixer_merge(z, p, gates, ya, sgu_norm[l], w_s[l].astype(BF16), b_s[l].T,
                             w_pool[l].astype(BF16), pool_scale[l])
        x = matmul_res(merged, w_o[l].astype(BF16), x, g1)

        hp, e_t, w_t, r_t, cnt = router(x, ffn_norm[l], sc2, sh2, w_router[l].T, b_router[l])
        counts = cnt[:, 0]
        padded = ((counts + MOE_ROWS - 1) // MOE_ROWS) * MOE_ROWS
        pad_end = jnp.cumsum(padded)
        pad_start = pad_end - padded
        n_used = (pad_end[-1:] // MOE_ROWS).astype(I32)
        block_e = jnp.minimum(
            jnp.searchsorted(pad_end, jnp.arange(n_blocks, dtype=I32) * MOE_ROWS, side='right'),
            N_EXPERTS - 1).astype(I32)
        dest_flat = dest_rows(pad_start.astype(I32), e_t, r_t).T.reshape(t * TOP_K)
        buf = dispatch(hp.reshape(t, d // 2), dest_flat, n_rows)
        ybuf = experts(buf, block_e, n_used, w_gate[l].astype(BF16), b_gate[l], w_up[l].astype(BF16), b_up[l],
                       w_down[l].astype(BF16), b_down[l])
        x = combine(dest_flat, w_t.T, ybuf, x, g2, final_norm, final=(l == n_layers - 1))
    return x
```

```python
import functools
import math

import jax
import jax.numpy as jnp
from jax import lax
from jax.experimental import pallas as pl
from jax.experimental.pallas import tpu as pltpu

F32 = jnp.float32
BF16 = jnp.bfloat16
U32 = jnp.uint32
I32 = jnp.int32

N_HEADS = 16
Q_LORA = 512
KV_LORA = 512
QK_NOPE = 128
QK_ROPE = 64
V_DIM = 128
ROPE_THETA = 10000.0
SGU_CHUNK = 128
SGU_GROUPS = 16
POOL_WINDOWS = (2, 4, 8, 16)
POOL_HALO = 16
N_EXPERTS = 32
TOP_K = 4
SWIGLU_LIMIT = 7.0
SWIGLU_ALPHA = 1.702
EPS = 1e-6
LANES = 128
MOE_ROWS = 256
FLASH_TQ = 2048
FLASH_TK = 1024
FLASH_SUB = 1024
MIB = 1024 * 1024


def _cparams(semantics, vmem_mib):
    return pltpu.CompilerParams(dimension_semantics=semantics, vmem_limit_bytes=vmem_mib * MIB)


def _adaln_kernel(c_ref, w_ref, b_ref, o_ref):
    c = c_ref[...]
    a = c * jax.nn.sigmoid(c)
    o_ref[...] = jnp.dot(a, w_ref[...], preferred_element_type=F32,
                         precision=lax.Precision.HIGHEST) + b_ref[...]


def adaln(c, w_ada, b_ada):
    n_layers, d, n = w_ada.shape
    rows = 8
    cp = jnp.zeros((rows, d), F32).at[:c.shape[0]].set(c)
    tn = 1024
    return pl.pallas_call(
        _adaln_kernel,
        out_shape=jax.ShapeDtypeStruct((n_layers, rows, n), F32),
        grid=(n_layers, n // tn),
        in_specs=[pl.BlockSpec((rows, d), lambda l, j: (0, 0)),
                  pl.BlockSpec((None, d, tn), lambda l, j: (l, 0, j)),
                  pl.BlockSpec((None, 1, tn), lambda l, j: (l, 0, j))],
        out_specs=pl.BlockSpec((None, rows, tn), lambda l, j: (l, 0, j)),
        compiler_params=_cparams(("parallel", "parallel"), 40),
        name="adaln",
    )(cp, w_ada, b_ada.reshape(n_layers, 1, n))


def _modulated_rmsnorm(x, gain, scale, shift):
    y = x * lax.rsqrt(jnp.mean(x * x, axis=-1, keepdims=True) + EPS) * gain
    return y * (1.0 + scale) + shift


def _norm_matmul_kernel(x_ref, g_ref, sc_ref, sh_ref, w_ref, o_ref, h_ref):
    @pl.when(pl.program_id(2) == 0)
    def _():
        h = _modulated_rmsnorm(x_ref[...], g_ref[...], sc_ref[...], sh_ref[...])
        h_ref[...] = h.astype(BF16)

    o_ref[...] = jnp.dot(h_ref[...], w_ref[...], preferred_element_type=F32)


def norm_matmul(x, gain, scale, shift, w, tn):
    b, s, d = x.shape
    n = w.shape[1]
    tm = min(1024, s)
    return pl.pallas_call(
        _norm_matmul_kernel,
        out_shape=jax.ShapeDtypeStruct((b, s, n), F32),
        grid=(b, s // tm, n // tn),
        in_specs=[pl.BlockSpec((None, tm, d), lambda bi, i, j: (bi, i, 0)),
                  pl.BlockSpec((1, d), lambda bi, i, j: (0, 0)),
                  pl.BlockSpec((None, 1, d), lambda bi, i, j: (bi, 0, 0)),
                  pl.BlockSpec((None, 1, d), lambda bi, i, j: (bi, 0, 0)),
                  pl.BlockSpec((d, tn), lambda bi, i, j: (0, j))],
        out_specs=pl.BlockSpec((None, tm, tn), lambda bi, i, j: (bi, i, j)),
        scratch_shapes=[pltpu.VMEM((tm, d), BF16)],
        compiler_params=_cparams(("parallel", "parallel", "arbitrary"), 56),
        name="norm_matmul",
    )(x, gain.reshape(1, d), scale, shift, w)


def _rms(x, gain):
    return x * lax.rsqrt(jnp.mean(x * x, axis=-1, keepdims=True) + EPS) * gain


def _rope_tile(t, cc, ss):
    return t * cc + pltpu.roll(t, LANES // 2, 1) * ss


def _mla_prep_kernel(lat_ref, qn_ref, kvn_ref, cc_ref, ss_ref, wq_ref, wkv_ref, q_ref, k_ref, v_ref):
    cc = cc_ref[...]
    ss = ss_ref[...]
    cq = _rms(lat_ref[:, :Q_LORA], qn_ref[...]).astype(BF16)
    ckv = _rms(lat_ref[:, Q_LORA:Q_LORA + KV_LORA], kvn_ref[...]).astype(BF16)
    kpe = _rope_tile(lat_ref[:, Q_LORA + KV_LORA:], cc, ss).astype(BF16)
    scale = (QK_NOPE + QK_ROPE) ** -0.5 * math.log2(math.e)
    for h in range(N_HEADS):
        qres = jnp.dot(cq, wq_ref[h], preferred_element_type=F32)
        q_ref[h, :, :QK_NOPE] = (qres[:, :QK_NOPE] * scale).astype(BF16)
        q_ref[h, :, QK_NOPE:] = (_rope_tile(qres[:, QK_NOPE:], cc, ss) * scale).astype(BF16)
        kvres = jnp.dot(ckv, wkv_ref[h], preferred_element_type=F32)
        k_ref[h, :, :QK_NOPE] = kvres[:, :QK_NOPE].astype(BF16)
        k_ref[h, :, QK_NOPE:] = kpe
        v_ref[h] = kvres[:, QK_NOPE:].astype(BF16)


def mla_prep(lat, q_norm, kv_norm, cc, ss, wq, wkv):
    b, s, nlat = lat.shape
    tm = min(512, s)
    hd = QK_NOPE + LANES
    return pl.pallas_call(
        _mla_prep_kernel,
        out_shape=(jax.ShapeDtypeStruct((b, N_HEADS, s, hd), BF16),
                   jax.ShapeDtypeStruct((b, N_HEADS, s, hd), BF16),
                   jax.ShapeDtypeStruct((b, N_HEADS, s, V_DIM), BF16)),
        grid=(b, s // tm),
        in_specs=[pl.BlockSpec((None, tm, nlat), lambda bi, i: (bi, i, 0)),
                  pl.BlockSpec((1, Q_LORA), lambda bi, i: (0, 0)),
                  pl.BlockSpec((1, KV_LORA), lambda bi, i: (0, 0)),
                  pl.BlockSpec((tm, LANES), lambda bi, i: (i, 0)),
                  pl.BlockSpec((tm, LANES), lambda bi, i: (i, 0)),
                  pl.BlockSpec((N_HEADS, Q_LORA, hd), lambda bi, i: (0, 0, 0)),
                  pl.BlockSpec((N_HEADS, KV_LORA, QK_NOPE + V_DIM), lambda bi, i: (0, 0, 0))],
        out_specs=(pl.BlockSpec((None, N_HEADS, tm, hd), lambda bi, i: (bi, 0, i, 0)),
                   pl.BlockSpec((None, N_HEADS, tm, hd), lambda bi, i: (bi, 0, i, 0)),
                   pl.BlockSpec((None, N_HEADS, tm, V_DIM), lambda bi, i: (bi, 0, i, 0))),
        compiler_params=_cparams(("parallel", "parallel"), 56),
        name="mla_prep",
    )(lat, q_norm.reshape(1, -1), kv_norm.reshape(1, -1), cc, ss, wq, wkv)


def _flash_kernel(q_ref, k_ref, v_ref, o_ref, m_ref, l_ref, acc_ref, *, tk, sub):
    qi = pl.program_id(2)
    tq = q_ref.shape[0]
    sub_tiles = tq // sub
    lane_tiles = tk // LANES
    m_ref[...] = jnp.full_like(m_ref, -jnp.inf)
    l_ref[...] = jnp.zeros_like(l_ref)
    acc_ref[...] = jnp.zeros_like(acc_ref)

    def attend(q_rows, kv_rows, diag_shift):
        s = lax.dot_general(q_ref[q_rows, :], k_ref[kv_rows, :], (((1,), (1,)), ((), ())),
                            preferred_element_type=F32)
        if diag_shift is not None:
            row = lax.broadcasted_iota(I32, s.shape, 0)
            col = lax.broadcasted_iota(I32, s.shape, 1) + diag_shift
            s = jnp.where(col <= row, s, -1e30)
        m_prev = m_ref[q_rows, :]
        m_next = jnp.maximum(m_prev, jnp.max(s, axis=-1, keepdims=True))
        alpha = jnp.exp2(m_prev - m_next)
        ps = [jnp.exp2(s[:, c * LANES:(c + 1) * LANES] - m_next) for c in range(lane_tiles)]
        psum = ps[0]
        for pc in ps[1:]:
            psum = psum + pc
        l_ref[q_rows, :] = alpha * l_ref[q_rows, :] + psum
        p = jnp.concatenate(ps, axis=1).astype(BF16)
        acc_ref[q_rows, :] = alpha * acc_ref[q_rows, :] + jnp.dot(p, v_ref[kv_rows, :],
                                                                  preferred_element_type=F32)
        m_ref[q_rows, :] = m_next

    def body(j, carry):
        kv_rows = pl.ds(pl.multiple_of(j * tk, tk), tk)
        for t in range(sub_tiles):
            attend(pl.ds(t * sub, sub), kv_rows, None)
        return carry

    lax.fori_loop(0, qi * (tq // tk), body, 0)
    for c in range(tq // tk):
        kv_rows = pl.ds(pl.multiple_of(qi * tq + c * tk, tk), tk)
        for t in range(sub_tiles):
            first_row, last_row = t * sub, (t + 1) * sub - 1
            first_col, last_col = c * tk, (c + 1) * tk - 1
            if first_col > last_row:
                continue
            visible = last_col <= first_row
            attend(pl.ds(t * sub, sub), kv_rows, None if visible else first_col - first_row)
    o_ref[...] = (acc_ref[...] / jnp.sum(l_ref[...], axis=-1, keepdims=True)).astype(o_ref.dtype)


def flash(q, k, v):
    b, h, s, hd = q.shape
    tq = min(FLASH_TQ, s)
    tk = min(FLASH_TK, tq)
    sub = min(FLASH_SUB, tq)
    return pl.pallas_call(
        functools.partial(_flash_kernel, tk=tk, sub=sub),
        out_shape=jax.ShapeDtypeStruct((b, s, h * V_DIM), BF16),
        grid=(b, h, s // tq),
        in_specs=[pl.BlockSpec((None, None, tq, hd), lambda bi, hi, qi: (bi, hi, qi, 0)),
                  pl.BlockSpec((None, None, s, hd), lambda bi, hi, qi: (bi, hi, 0, 0)),
                  pl.BlockSpec((None, None, s, V_DIM), lambda bi, hi, qi: (bi, hi, 0, 0))],
        out_specs=pl.BlockSpec((None, tq, V_DIM), lambda bi, hi, qi: (bi, qi, hi)),
        scratch_shapes=[pltpu.VMEM((tq, LANES), F32), pltpu.VMEM((tq, LANES), F32),
                        pltpu.VMEM((tq, V_DIM), F32)],
        compiler_params=_cparams(("parallel", "parallel", "arbitrary"), 48),
        name="flash",
    )(q, k, v)


def _gelu_tanh(x):
    return 0.5 * x * (1.0 + jnp.tanh(math.sqrt(2.0 / math.pi) * (x + 0.044715 * (x * x * x))))


def _mixer_merge_kernel(z_ref, p_ref, halo_ref, gt_ref, ya_ref, sn_ref, ws_ref, bs_ref,
                        wp_ref, ps_ref, o_ref, acc_ref):
    tm, d = o_ref.shape
    gd = d // SGU_GROUPS
    tri_r = lax.broadcasted_iota(I32, (SGU_CHUNK, SGU_CHUNK), 0)
    tri_c = lax.broadcasted_iota(I32, (SGU_CHUNK, SGU_CHUNK), 1)
    tril = tri_c <= tri_r

    acc_ref[...] = jax.nn.sigmoid(gt_ref[:, :d]) * ya_ref[...].astype(F32)

    for c in range(tm // SGU_CHUNK):
        rows = pl.ds(c * SGU_CHUNK, SGU_CHUNK)
        v = _gelu_tanh(z_ref[rows, d:])
        mu = jnp.mean(v, axis=-1, keepdims=True)
        vc = v - mu
        var = jnp.mean(vc * vc, axis=-1, keepdims=True)
        vn = (vc * lax.rsqrt(var + EPS) * sn_ref[...]).astype(BF16)
        for g in range(SGU_GROUPS):
            cols = pl.ds(g * gd, gd)
            w = jnp.where(tril, ws_ref[g], jnp.zeros((), BF16))
            mixed = jnp.dot(w, vn[:, g * gd:(g + 1) * gd], preferred_element_type=F32) + bs_ref[:, g:g + 1]
            yb = _gelu_tanh(z_ref[rows, cols]) * mixed
            gate = jax.nn.sigmoid(gt_ref[rows, pl.ds(d + g * gd, gd)])
            acc_ref[rows, cols] = acc_ref[rows, cols] + gate * yb

    pgd = d // len(POOL_WINDOWS)
    first = pl.program_id(1) == 0
    pos = (pl.program_id(1) * tm + lax.broadcasted_iota(I32, (tm, 1), 0) + 1).astype(F32)
    for gi, win in enumerate(POOL_WINDOWS):
        cols = pl.ds(gi * pgd, pgd)
        xg = p_ref[:, cols]
        halo = jnp.where(first, 0.0, halo_ref[:, cols])
        e = jnp.concatenate([halo, xg], axis=0)
        span = 1
        while span < win:
            e = e + pltpu.roll(e, span, 0)
            span *= 2
        wsum = e[POOL_HALO:]
        dlt = wsum / jnp.minimum(pos, float(win)) - xg
        yc = jnp.dot(dlt.astype(BF16), wp_ref[gi], preferred_element_type=F32) * ps_ref[:, cols]
        gate = jax.nn.sigmoid(gt_ref[:, pl.ds(2 * d + gi * pgd, pgd)])
        o_ref[:, cols] = (acc_ref[:, cols] + gate * yc).astype(o_ref.dtype)


def mixer_merge(z, p, gates, ya, sgu_norm, ws, bs_t, wp, pool_scale):
    b, s, d = p.shape
    tm = min(256, s)
    hb = tm // POOL_HALO
    return pl.pallas_call(
        _mixer_merge_kernel,
        out_shape=jax.ShapeDtypeStruct((b, s, d), BF16),
        grid=(b, s // tm),
        in_specs=[pl.BlockSpec((None, tm, 2 * d), lambda bi, i: (bi, i, 0)),
                  pl.BlockSpec((None, tm, d), lambda bi, i: (bi, i, 0)),
                  pl.BlockSpec((None, POOL_HALO, d), lambda bi, i: (bi, jnp.maximum(i * hb - 1, 0), 0)),
                  pl.BlockSpec((None, tm, 3 * d), lambda bi, i: (bi, i, 0)),
                  pl.BlockSpec((None, tm, d), lambda bi, i: (bi, i, 0)),
                  pl.BlockSpec((1, d), lambda bi, i: (0, 0)),
                  pl.BlockSpec(ws.shape, lambda bi, i: (0, 0, 0)),
                  pl.BlockSpec(bs_t.shape, lambda bi, i: (0, 0)),
                  pl.BlockSpec(wp.shape, lambda bi, i: (0, 0, 0)),
                  pl.BlockSpec((1, d), lambda bi, i: (0, 0))],
        out_specs=pl.BlockSpec((None, tm, d), lambda bi, i: (bi, i, 0)),
        scratch_shapes=[pltpu.VMEM((tm, d), F32)],
        compiler_params=_cparams(("parallel", "arbitrary"), 56),
        name="mixer_merge",
    )(z, p, p, gates, ya, sgu_norm.reshape(1, d), ws, bs_t, wp, pool_scale.reshape(1, d))


def _matmul_res_kernel(a_ref, w_ref, x_ref, g_ref, o_ref):
    y = jnp.dot(a_ref[...], w_ref[...], preferred_element_type=F32)
    o_ref[...] = x_ref[...] + g_ref[...] * y


def matmul_res(a, w, x, gate):
    b, s, d = x.shape
    tm = min(512, s)
    return pl.pallas_call(
        _matmul_res_kernel,
        out_shape=jax.ShapeDtypeStruct((b, s, d), F32),
        grid=(b, s // tm),
        in_specs=[pl.BlockSpec((None, tm, d), lambda bi, i: (bi, i, 0)),
                  pl.BlockSpec((d, d), lambda bi, i: (0, 0)),
                  pl.BlockSpec((None, tm, d), lambda bi, i: (bi, i, 0)),
                  pl.BlockSpec((None, 1, d), lambda bi, i: (bi, 0, 0))],
        out_specs=pl.BlockSpec((None, tm, d), lambda bi, i: (bi, i, 0)),
        compiler_params=_cparams(("parallel", "parallel"), 56),
        name="matmul_res",
    )(a, w, x, gate)


def _pack_bf16_pair(a, b):
    pa = lax.bitcast_convert_type(a.astype(BF16).astype(F32), U32)
    pb = lax.bitcast_convert_type(b.astype(BF16).astype(F32), U32)
    return pa | (pb >> 16)


def _unpack_bf16_pair(w):
    hi = lax.bitcast_convert_type(w & jnp.uint32(0xFFFF0000), F32)
    lo = lax.bitcast_convert_type(w << 16, F32)
    return hi.astype(BF16), lo.astype(BF16)


def _router_kernel(x_ref, g_ref, sc_ref, sh_ref, wr_ref, br_ref,
                   hp_ref, e_ref, w_ref, r_ref, cnt_ref, carry_ref):
    tm, d = x_ref.shape
    half = d // 2

    @pl.when((pl.program_id(0) == 0) & (pl.program_id(1) == 0))
    def _():
        carry_ref[...] = jnp.zeros_like(carry_ref)

    h = _modulated_rmsnorm(x_ref[...], g_ref[...], sc_ref[...], sh_ref[...])
    hp_ref[...] = _pack_bf16_pair(h[:, :half], h[:, half:])

    logits = lax.dot_general(wr_ref[...], h, (((1,), (1,)), ((), ())),
                             preferred_element_type=F32, precision=lax.Precision.HIGHEST) + br_ref[...]
    eio = lax.broadcasted_iota(I32, logits.shape, 0).astype(F32)
    onehot = jnp.zeros(logits.shape, F32)
    vals, sels = [], []
    rem = logits
    for k in range(TOP_K):
        m = jnp.max(rem, axis=0, keepdims=True)
        idx = jnp.min(jnp.where(rem == m, eio, float(N_EXPERTS)), axis=0, keepdims=True)
        sel = eio == idx
        rem = jnp.where(sel, -jnp.inf, rem)
        onehot = onehot + jnp.where(sel, 1.0, 0.0)
        vals.append(m)
        sels.append(sel)
        e_ref[k:k + 1, :] = idx.astype(I32)
    ex = [jnp.exp(v - vals[0]) for v in vals]
    den = ex[0] + ex[1] + ex[2] + ex[3]
    for k in range(TOP_K):
        w_ref[k:k + 1, :] = ex[k] / den

    before = lax.broadcasted_iota(I32, (tm, tm), 0) < lax.broadcasted_iota(I32, (tm, tm), 1)
    cum = jnp.dot(onehot.astype(BF16), jnp.where(before, 1.0, 0.0).astype(BF16),
                  preferred_element_type=F32) + carry_ref[...]
    for k in range(TOP_K):
        r_ref[k:k + 1, :] = jnp.sum(jnp.where(sels[k], cum, 0.0), axis=0, keepdims=True).astype(I32)
    carry_ref[...] = carry_ref[...] + jnp.sum(onehot, axis=1, keepdims=True)
    cnt_ref[...] = jnp.broadcast_to(carry_ref[...], cnt_ref.shape).astype(I32)


def router(x, gain, scale, shift, wr_t, b_router):
    b, s, d = x.shape
    tm = min(512, s)
    nt = s // tm
    t = b * s
    tok = lambda bi, i: (0, bi * nt + i)
    return pl.pallas_call(
        _router_kernel,
        out_shape=(jax.ShapeDtypeStruct((b, s, d // 2), U32),
                   jax.ShapeDtypeStruct((TOP_K, t), I32),
                   jax.ShapeDtypeStruct((TOP_K, t), F32),
                   jax.ShapeDtypeStruct((TOP_K, t), I32),
                   jax.ShapeDtypeStruct((N_EXPERTS, LANES), I32)),
        grid=(b, nt),
        in_specs=[pl.BlockSpec((None, tm, d), lambda bi, i: (bi, i, 0)),
                  pl.BlockSpec((1, d), lambda bi, i: (0, 0)),
                  pl.BlockSpec((None, 1, d), lambda bi, i: (bi, 0, 0)),
                  pl.BlockSpec((None, 1, d), lambda bi, i: (bi, 0, 0)),
                  pl.BlockSpec((N_EXPERTS, d), lambda bi, i: (0, 0)),
                  pl.BlockSpec((N_EXPERTS, 1), lambda bi, i: (0, 0))],
        out_specs=(pl.BlockSpec((None, tm, d // 2), lambda bi, i: (bi, i, 0)),
                   pl.BlockSpec((TOP_K, tm), tok),
                   pl.BlockSpec((TOP_K, tm), tok),
                   pl.BlockSpec((TOP_K, tm), tok),
                   pl.BlockSpec((N_EXPERTS, LANES), lambda bi, i: (0, 0))),
        scratch_shapes=[pltpu.VMEM((N_EXPERTS, 1), F32)],
        compiler_params=_cparams(("arbitrary", "arbitrary"), 40),
        name="router",
    )(x, gain.reshape(1, d), scale, shift, wr_t, b_router.reshape(N_EXPERTS, 1))


def _dest_kernel(ps_ref, e_ref, r_ref, o_ref):
    e = e_ref[...]
    base = jnp.zeros(e.shape, I32)
    for j in range(N_EXPERTS):
        base = jnp.where(e == j, ps_ref[j], base)
    o_ref[...] = base + r_ref[...]


def dest_rows(pad_start, e_t, r_t):
    return pl.pallas_call(
        _dest_kernel,
        out_shape=jax.ShapeDtypeStruct(e_t.shape, I32),
        grid_spec=pltpu.PrefetchScalarGridSpec(
            num_scalar_prefetch=1, grid=(1,),
            in_specs=[pl.BlockSpec(e_t.shape, lambda i, ps: (0, 0)),
                      pl.BlockSpec(e_t.shape, lambda i, ps: (0, 0))],
            out_specs=pl.BlockSpec(e_t.shape, lambda i, ps: (0, 0))),
        compiler_params=_cparams(("arbitrary",), 16),
        name="dest",
    )(pad_start, e_t, r_t)


def _dispatch_kernel(dest_ref, h_ref, buf_in_ref, buf_ref, sem):
    del buf_in_ref
    tm = h_ref.shape[0]

    def row_copy(r, d):
        return pltpu.make_async_copy(h_ref.at[r], buf_ref.at[d], sem)

    def issue(r, carry):
        for k in range(TOP_K):
            row_copy(r, dest_ref[r * TOP_K + k]).start()
        return carry

    lax.fori_loop(0, tm, issue, 0)

    def drain(j, carry):
        row_copy(0, 0).wait()
        return carry

    lax.fori_loop(0, tm * TOP_K, drain, 0)


def dispatch(hp, dest_flat, n_rows):
    t, w = hp.shape
    tm = min(256, t)
    row_shape = (w // LANES, LANES)
    buf0 = jnp.zeros((n_rows,) + row_shape, hp.dtype)
    buf = pl.pallas_call(
        _dispatch_kernel,
        out_shape=jax.ShapeDtypeStruct((n_rows,) + row_shape, hp.dtype),
        grid=(t // tm,),
        in_specs=[pl.BlockSpec((tm * TOP_K,), lambda i: (i,), memory_space=pltpu.SMEM),
                  pl.BlockSpec((tm,) + row_shape, lambda i: (i, 0, 0)),
                  pl.BlockSpec(memory_space=pl.ANY)],
        out_specs=pl.BlockSpec(memory_space=pl.ANY),
        scratch_shapes=[pltpu.SemaphoreType.DMA],
        input_output_aliases={2: 0},
        compiler_params=_cparams(("arbitrary",), 16),
        name="dispatch",
    )(dest_flat, hp.reshape((t,) + row_shape), buf0)
    return buf.reshape(n_rows, w)


def _experts_kernel(be_ref, nu_ref, x_ref, wg_ref, bg_ref, wu_ref, bu_ref, wd_ref, bd_ref, o_ref):
    del be_ref

    @pl.when(pl.program_id(0) < nu_ref[0])
    def _():
        hi, lo = _unpack_bf16_pair(x_ref[...])
        x = jnp.concatenate([hi, lo], axis=1)
        gl = jnp.dot(x, wg_ref[...], preferred_element_type=F32) + bg_ref[...]
        li = jnp.dot(x, wu_ref[...], preferred_element_type=F32) + bu_ref[...]
        gl = jnp.minimum(gl, SWIGLU_LIMIT)
        li = jnp.clip(li, -SWIGLU_LIMIT, SWIGLU_LIMIT)
        act = gl * jax.nn.sigmoid(SWIGLU_ALPHA * gl) * (li + 1.0)
        o_ref[...] = jnp.dot(act.astype(BF16), wd_ref[...], preferred_element_type=F32) + bd_ref[...]

    @pl.when(pl.program_id(0) >= nu_ref[0])
    def _():
        o_ref[...] = jnp.zeros_like(o_ref)


def experts(buf, block_e, n_used, wg, bg, wu, bu, wd, bd):
    n_rows, w = buf.shape
    n_blocks = n_rows // MOE_ROWS
    n_e, d, f = wg.shape
    row_idx = lambda i, be, nu: (jnp.minimum(i, nu[0] - 1), 0)
    return pl.pallas_call(
        _experts_kernel,
        out_shape=jax.ShapeDtypeStruct((n_rows, d), F32),
        grid_spec=pltpu.PrefetchScalarGridSpec(
            num_scalar_prefetch=2, grid=(n_blocks,),
            in_specs=[pl.BlockSpec((MOE_ROWS, w), row_idx),
                      pl.BlockSpec((None, d, f), lambda i, be, nu: (be[i], 0, 0)),
                      pl.BlockSpec((None, 1, f), lambda i, be, nu: (be[i], 0, 0)),
                      pl.BlockSpec((None, d, f), lambda i, be, nu: (be[i], 0, 0)),
                      pl.BlockSpec((None, 1, f), lambda i, be, nu: (be[i], 0, 0)),
                      pl.BlockSpec((None, f, d), lambda i, be, nu: (be[i], 0, 0)),
                      pl.BlockSpec((None, 1, d), lambda i, be, nu: (be[i], 0, 0))],
            out_specs=pl.BlockSpec((MOE_ROWS, d), lambda i, be, nu: (i, 0))),
        compiler_params=_cparams(("arbitrary",), 56),
        name="experts",
    )(block_e, n_used, buf, wg, bg.reshape(n_e, 1, f), wu, bu.reshape(n_e, 1, f), wd, bd.reshape(n_e, 1, d))


def _combine_kernel(dest_ref, w_ref, y_ref, x_ref, g_ref, fn_ref, o_ref, rows_ref, sem, *, final):
    tm = x_ref.shape[0]

    def row_copy(d, k, r):
        return pltpu.make_async_copy(y_ref.at[pl.ds(d, 1)], rows_ref.at[k, pl.ds(r, 1)], sem)

    def issue(r, carry):
        for k in range(TOP_K):
            row_copy(dest_ref[r * TOP_K + k], k, r).start()
        return carry

    lax.fori_loop(0, tm, issue, 0)

    def drain(j, carry):
        row_copy(0, 0, 0).wait()
        return carry

    lax.fori_loop(0, tm * TOP_K, drain, 0)

    y = w_ref[:, 0:1] * rows_ref[0]
    for k in range(1, TOP_K):
        y = y + w_ref[:, k:k + 1] * rows_ref[k]
    out = x_ref[...] + g_ref[...] * y
    if final:
        out = _rms(out, fn_ref[...])
    o_ref[...] = out


def combine(dest_flat, w_tok, ybuf, x, gate, final_norm, final):
    b, s, d = x.shape
    tm = min(256, s)
    nt = s // tm
    return pl.pallas_call(
        functools.partial(_combine_kernel, final=final),
        out_shape=jax.ShapeDtypeStruct((b, s, d), F32),
        grid=(b, nt),
        in_specs=[pl.BlockSpec((tm * TOP_K,), lambda bi, i: (bi * nt + i,), memory_space=pltpu.SMEM),
                  pl.BlockSpec((tm, TOP_K), lambda bi, i: (bi * nt + i, 0)),
                  pl.BlockSpec(memory_space=pl.ANY),
                  pl.BlockSpec((None, tm, d), lambda bi, i: (bi, i, 0)),
                  pl.BlockSpec((None, 1, d), lambda bi, i: (bi, 0, 0)),
                  pl.BlockSpec((1, d), lambda bi, i: (0, 0))],
        out_specs=pl.BlockSpec((None, tm, d), lambda bi, i: (bi, i, 0)),
        scratch_shapes=[pltpu.VMEM((TOP_K, tm, d), F32), pltpu.SemaphoreType.DMA],
        compiler_params=_cparams(("arbitrary", "arbitrary"), 40),
        name="combine",
    )(dest_flat, w_tok, ybuf, x, gate, final_norm.reshape(1, d))


def _swap_halves(w):
    half = w.shape[-1] // 2
    return jnp.concatenate([w[..., half:], w[..., :half]], axis=-1)


def _rope_tables(s):
    pos = jnp.arange(s, dtype=F32)
    inv_freq = ROPE_THETA ** (-jnp.arange(0, QK_ROPE, 2, dtype=F32) / QK_ROPE)
    ang = pos[:, None] * inv_freq[None, :]
    cos, sin = jnp.cos(ang), jnp.sin(ang)
    zero = jnp.zeros((s, LANES - QK_ROPE), F32)
    return (jnp.concatenate([cos, cos, zero], axis=1), jnp.concatenate([-sin, sin, zero], axis=1))


def kernel(x, c, attn_norm, ffn_norm, w_ada, b_ada, w_in, q_norm, kv_norm, w_uq, w_ukv, sgu_norm, w_s, b_s,
           w_pool, pool_scale, w_o, w_router, b_router, w_gate, b_gate, w_up, b_up, w_down, b_down, final_norm):
    b, s, d = x.shape
    n_layers = w_ada.shape[0]
    t = b * s
    mla_in = Q_LORA + KV_LORA + QK_ROPE
    mods = adaln(c, w_ada, b_ada)
    cc, ss = _rope_tables(s)
    n_blocks = t * TOP_K // MOE_ROWS + N_EXPERTS
    n_rows = n_blocks * MOE_ROWS

    for l in range(n_layers):
        sh1, sc1, g1, sh2, sc2, g2 = [mods[l, :b, i * d:(i + 1) * d].reshape(b, 1, d) for i in range(6)]

        wl = w_in[l]
        k_rope_cols = wl[:, Q_LORA + KV_LORA:mla_in]
        w_lat = jnp.concatenate([wl[:, :mla_in], _swap_halves(k_rope_cols)], axis=1).astype(BF16)
        w_z = wl[:, mla_in:mla_in + 2 * d].astype(BF16)
        w_p = wl[:, mla_in + 2 * d:mla_in + 3 * d].astype(BF16)
        w_g = wl[:, mla_in + 3 * d:].astype(BF16)
        lat = norm_matmul(x, attn_norm[l], sc1, sh1, w_lat, w_lat.shape[1])
        z = norm_matmul(x, attn_norm[l], sc1, sh1, w_z, 1024)
        p = norm_matmul(x, attn_norm[l], sc1, sh1, w_p, 1024)
        gates = norm_matmul(x, attn_norm[l], sc1, sh1, w_g, 1024)

        wq = w_uq[l]
        wq = jnp.concatenate([wq, _swap_halves(wq[..., QK_NOPE:])], axis=-1)
        wq = wq.transpose(1, 0, 2).astype(BF16)
        wkv = w_ukv[l].transpose(1, 0, 2).astype(BF16)
        q, k, v = mla_prep(lat, q_norm[l], kv_norm[l], cc, ss, wq, wkv)
        ya = flash(q, k, v)

        merged = mixer_merge(z, p, gates, ya, sgu_norm[l], w_s[l].astype(BF16), b_s[l].T,
                             w_pool[l].astype(BF16), pool_scale[l])
        x = matmul_res(merged, w_o[l].astype(BF16), x, g1)

        hp, e_t, w_t, r_t, cnt = router(x, ffn_norm[l], sc2, sh2, w_router[l].T, b_router[l])
        counts = cnt[:, 0]
        padded = ((counts + MOE_ROWS - 1) // MOE_ROWS) * MOE_ROWS
        pad_end = jnp.cumsum(padded)
        pad_start = pad_end - padded
        n_used = (pad_end[-1:] // MOE_ROWS).astype(I32)
        block_start = jnp.arange(n_blocks, dtype=I32) * MOE_ROWS
        block_e = jnp.minimum(jnp.sum((pad_end[None, :] <= block_start[:, None]).astype(I32), axis=1),
                              N_EXPERTS - 1)
        dest_flat = dest_rows(pad_start.astype(I32), e_t, r_t).T.reshape(t * TOP_K)
        buf = dispatch(hp.reshape(t, d // 2), dest_flat, n_rows)
        ybuf = experts(buf, block_e, n_used, w_gate[l].astype(BF16), b_gate[l], w_up[l].astype(BF16), b_up[l],
                       w_down[l].astype(BF16), b_down[l])
        x = combine(dest_flat, w_t.T, ybuf, x, g2, final_norm, final=(l == n_layers - 1))
    return x
```

```python
import functools
import math

import jax
import jax.numpy as jnp
from jax import lax
from jax.experimental import pallas as pl
from jax.experimental.pallas import tpu as pltpu

F32 = jnp.float32
BF16 = jnp.bfloat16
U32 = jnp.uint32
I32 = jnp.int32

N_HEADS = 16
Q_LORA = 512
KV_LORA = 512
QK_NOPE = 128
QK_ROPE = 64
V_DIM = 128
ROPE_THETA = 10000.0
SGU_CHUNK = 128
SGU_GROUPS = 16
POOL_WINDOWS = (2, 4, 8, 16)
POOL_HALO = 16
N_EXPERTS = 32
TOP_K = 4
SWIGLU_LIMIT = 7.0
SWIGLU_ALPHA = 1.702
EPS = 1e-6
LANES = 128
MOE_ROWS = 256
FLASH_TQ = 2048
FLASH_TK = 1024
FLASH_SUB = 1024
MIB = 1024 * 1024


def _cparams(semantics, vmem_mib):
    return pltpu.CompilerParams(dimension_semantics=semantics, vmem_limit_bytes=vmem_mib * MIB)


def _adaln_kernel(c_ref, w_ref, b_ref, o_ref):
    c = c_ref[...]
    a = c * jax.nn.sigmoid(c)
    o_ref[...] = jnp.dot(a, w_ref[...], preferred_element_type=F32,
                         precision=lax.Precision.HIGHEST) + b_ref[...]


def adaln(c, w_ada, b_ada):
    n_layers, d, n = w_ada.shape
    rows = 8
    cp = jnp.zeros((rows, d), F32).at[:c.shape[0]].set(c)
    tn = 1024
    return pl.pallas_call(
        _adaln_kernel,
        out_shape=jax.ShapeDtypeStruct((n_layers, rows, n), F32),
        grid=(n_layers, n // tn),
        in_specs=[pl.BlockSpec((rows, d), lambda l, j: (0, 0)),
                  pl.BlockSpec((None, d, tn), lambda l, j: (l, 0, j)),
                  pl.BlockSpec((None, 1, tn), lambda l, j: (l, 0, j))],
        out_specs=pl.BlockSpec((None, rows, tn), lambda l, j: (l, 0, j)),
        compiler_params=_cparams(("parallel", "parallel"), 40),
        name="adaln",
    )(cp, w_ada, b_ada.reshape(n_layers, 1, n))


def _modulated_rmsnorm(x, gain, scale, shift):
    y = x * lax.rsqrt(jnp.mean(x * x, axis=-1, keepdims=True) + EPS) * gain
    return y * (1.0 + scale) + shift


def _gelu_tanh(x):
    return 0.5 * x * (1.0 + jnp.tanh(math.sqrt(2.0 / math.pi) * (x + 0.044715 * (x * x * x))))


_EPILOGUES = {None: lambda y: y, "gelu": _gelu_tanh, "sigmoid": jax.nn.sigmoid}


def _norm_matmul_kernel(x_ref, g_ref, sc_ref, sh_ref, w_ref, o_ref, h_ref, *, epilogue):
    @pl.when(pl.program_id(2) == 0)
    def _():
        h = _modulated_rmsnorm(x_ref[...], g_ref[...], sc_ref[...], sh_ref[...])
        h_ref[...] = h.astype(BF16)

    y = jnp.dot(h_ref[...], w_ref[...], preferred_element_type=F32)
    o_ref[...] = _EPILOGUES[epilogue](y).astype(o_ref.dtype)


def norm_matmul(x, gain, scale, shift, w, tn, epilogue=None, out_dtype=F32):
    b, s, d = x.shape
    n = w.shape[1]
    tm = min(1024, s)
    return pl.pallas_call(
        functools.partial(_norm_matmul_kernel, epilogue=epilogue),
        out_shape=jax.ShapeDtypeStruct((b, s, n), out_dtype),
        grid=(b, s // tm, n // tn),
        in_specs=[pl.BlockSpec((None, tm, d), lambda bi, i, j: (bi, i, 0)),
                  pl.BlockSpec((1, d), lambda bi, i, j: (0, 0)),
                  pl.BlockSpec((None, 1, d), lambda bi, i, j: (bi, 0, 0)),
                  pl.BlockSpec((None, 1, d), lambda bi, i, j: (bi, 0, 0)),
                  pl.BlockSpec((d, tn), lambda bi, i, j: (0, j))],
        out_specs=pl.BlockSpec((None, tm, tn), lambda bi, i, j: (bi, i, j)),
        scratch_shapes=[pltpu.VMEM((tm, d), BF16)],
        compiler_params=_cparams(("parallel", "parallel", "arbitrary"), 56),
        name="norm_matmul",
    )(x, gain.reshape(1, d), scale, shift, w)


def _rms(x, gain):
    return x * lax.rsqrt(jnp.mean(x * x, axis=-1, keepdims=True) + EPS) * gain


def _rope_tile(t, cc, ss):
    return t * cc + pltpu.roll(t, LANES // 2, 1) * ss


def _mla_prep_kernel(lat_ref, qn_ref, kvn_ref, cc_ref, ss_ref, wq_ref, wkv_ref, q_ref, k_ref, v_ref):
    cc = cc_ref[...]
    ss = ss_ref[...]
    cq = _rms(lat_ref[:, :Q_LORA], qn_ref[...]).astype(BF16)
    ckv = _rms(lat_ref[:, Q_LORA:Q_LORA + KV_LORA], kvn_ref[...]).astype(BF16)
    kpe = _rope_tile(lat_ref[:, Q_LORA + KV_LORA:], cc, ss).astype(BF16)
    scale = (QK_NOPE + QK_ROPE) ** -0.5 * math.log2(math.e)
    for h in range(N_HEADS):
        qres = jnp.dot(cq, wq_ref[h], preferred_element_type=F32)
        q_ref[h, :, :QK_NOPE] = (qres[:, :QK_NOPE] * scale).astype(BF16)
        q_ref[h, :, QK_NOPE:] = (_rope_tile(qres[:, QK_NOPE:], cc, ss) * scale).astype(BF16)
        kvres = jnp.dot(ckv, wkv_ref[h], preferred_element_type=F32)
        k_ref[h, :, :QK_NOPE] = kvres[:, :QK_NOPE].astype(BF16)
        k_ref[h, :, QK_NOPE:] = kpe
        v_ref[h] = kvres[:, QK_NOPE:].astype(BF16)


def mla_prep(lat, q_norm, kv_norm, cc, ss, wq, wkv):
    b, s, nlat = lat.shape
    tm = min(512, s)
    hd = QK_NOPE + LANES
    return pl.pallas_call(
        _mla_prep_kernel,
        out_shape=(jax.ShapeDtypeStruct((b, N_HEADS, s, hd), BF16),
                   jax.ShapeDtypeStruct((b, N_HEADS, s, hd), BF16),
                   jax.ShapeDtypeStruct((b, N_HEADS, s, V_DIM), BF16)),
        grid=(b, s // tm),
        in_specs=[pl.BlockSpec((None, tm, nlat), lambda bi, i: (bi, i, 0)),
                  pl.BlockSpec((1, Q_LORA), lambda bi, i: (0, 0)),
                  pl.BlockSpec((1, KV_LORA), lambda bi, i: (0, 0)),
                  pl.BlockSpec((tm, LANES), lambda bi, i: (i, 0)),
                  pl.BlockSpec((tm, LANES), lambda bi, i: (i, 0)),
                  pl.BlockSpec((N_HEADS, Q_LORA, hd), lambda bi, i: (0, 0, 0)),
                  pl.BlockSpec((N_HEADS, KV_LORA, QK_NOPE + V_DIM), lambda bi, i: (0, 0, 0))],
        out_specs=(pl.BlockSpec((None, N_HEADS, tm, hd), lambda bi, i: (bi, 0, i, 0)),
                   pl.BlockSpec((None, N_HEADS, tm, hd), lambda bi, i: (bi, 0, i, 0)),
                   pl.BlockSpec((None, N_HEADS, tm, V_DIM), lambda bi, i: (bi, 0, i, 0))),
        compiler_params=_cparams(("parallel", "parallel"), 56),
        name="mla_prep",
    )(lat, q_norm.reshape(1, -1), kv_norm.reshape(1, -1), cc, ss, wq, wkv)


def _flash_kernel(q_ref, k_ref, v_ref, o_ref, m_ref, l_ref, acc_ref, *, tk, sub):
    qi = pl.program_id(2)
    tq = q_ref.shape[0]
    sub_tiles = tq // sub
    lane_tiles = tk // LANES
    m_ref[...] = jnp.full_like(m_ref, -jnp.inf)
    l_ref[...] = jnp.zeros_like(l_ref)
    acc_ref[...] = jnp.zeros_like(acc_ref)

    def attend(q_rows, kv_rows, diag_shift):
        s = lax.dot_general(q_ref[q_rows, :], k_ref[kv_rows, :], (((1,), (1,)), ((), ())),
                            preferred_element_type=F32)
        if diag_shift is not None:
            row = lax.broadcasted_iota(I32, s.shape, 0)
            col = lax.broadcasted_iota(I32, s.shape, 1) + diag_shift
            s = jnp.where(col <= row, s, -1e30)
        m_prev = m_ref[q_rows, :]
        m_next = jnp.maximum(m_prev, jnp.max(s, axis=-1, keepdims=True))
        alpha = jnp.exp2(m_prev - m_next)
        ps = [jnp.exp2(s[:, c * LANES:(c + 1) * LANES] - m_next) for c in range(lane_tiles)]
        psum = ps[0]
        for pc in ps[1:]:
            psum = psum + pc
        l_ref[q_rows, :] = alpha * l_ref[q_rows, :] + psum
        p = jnp.concatenate(ps, axis=1).astype(BF16)
        acc_ref[q_rows, :] = alpha * acc_ref[q_rows, :] + jnp.dot(p, v_ref[kv_rows, :],
                                                                  preferred_element_type=F32)
        m_ref[q_rows, :] = m_next

    def body(j, carry):
        kv_rows = pl.ds(pl.multiple_of(j * tk, tk), tk)
        for t in range(sub_tiles):
            attend(pl.ds(t * sub, sub), kv_rows, None)
        return carry

    lax.fori_loop(0, qi * (tq // tk), body, 0)
    for c in range(tq // tk):
        kv_rows = pl.ds(pl.multiple_of(qi * tq + c * tk, tk), tk)
        for t in range(sub_tiles):
            first_row, last_row = t * sub, (t + 1) * sub - 1
            first_col, last_col = c * tk, (c + 1) * tk - 1
            if first_col > last_row:
                continue
            visible = last_col <= first_row
            attend(pl.ds(t * sub, sub), kv_rows, None if visible else first_col - first_row)
    o_ref[...] = (acc_ref[...] / jnp.sum(l_ref[...], axis=-1, keepdims=True)).astype(o_ref.dtype)


def flash(q, k, v):
    b, h, s, hd = q.shape
    tq = min(FLASH_TQ, s)
    tk = min(FLASH_TK, tq)
    sub = min(FLASH_SUB, tq)
    return pl.pallas_call(
        functools.partial(_flash_kernel, tk=tk, sub=sub),
        out_shape=jax.ShapeDtypeStruct((b, s, h * V_DIM), BF16),
        grid=(b, h, s // tq),
        in_specs=[pl.BlockSpec((None, None, tq, hd), lambda bi, hi, qi: (bi, hi, qi, 0)),
                  pl.BlockSpec((None, None, s, hd), lambda bi, hi, qi: (bi, hi, 0, 0)),
                  pl.BlockSpec((None, None, s, V_DIM), lambda bi, hi, qi: (bi, hi, 0, 0))],
        out_specs=pl.BlockSpec((None, tq, V_DIM), lambda bi, hi, qi: (bi, qi, hi)),
        scratch_shapes=[pltpu.VMEM((tq, LANES), F32), pltpu.VMEM((tq, LANES), F32),
                        pltpu.VMEM((tq, V_DIM), F32)],
        compiler_params=_cparams(("parallel", "parallel", "arbitrary"), 48),
        name="flash",
    )(q, k, v)


def _mixer_merge_kernel(z_ref, p_ref, halo_ref, gt_ref, ya_ref, sn_ref, ws_ref, bs_ref,
                        wp_ref, ps_ref, o_ref, acc_ref):
    tm, d = o_ref.shape
    gd = d // SGU_GROUPS
    tri_r = lax.broadcasted_iota(I32, (SGU_CHUNK, SGU_CHUNK), 0)
    tri_c = lax.broadcasted_iota(I32, (SGU_CHUNK, SGU_CHUNK), 1)
    tril = tri_c <= tri_r

    acc_ref[...] = gt_ref[:, :d].astype(F32) * ya_ref[...].astype(F32)

    for c in range(tm // SGU_CHUNK):
        rows = pl.ds(c * SGU_CHUNK, SGU_CHUNK)
        v = z_ref[rows, d:].astype(F32)
        mu = jnp.mean(v, axis=-1, keepdims=True)
        vc = v - mu
        var = jnp.mean(vc * vc, axis=-1, keepdims=True)
        vn = (vc * lax.rsqrt(var + EPS) * sn_ref[...]).astype(BF16)
        for g in range(SGU_GROUPS):
            cols = pl.ds(g * gd, gd)
            w = jnp.where(tril, ws_ref[g], jnp.zeros((), BF16))
            mixed = jnp.dot(w, vn[:, g * gd:(g + 1) * gd], preferred_element_type=F32) + bs_ref[:, g:g + 1]
            yb = z_ref[rows, cols].astype(F32) * mixed
            gate = gt_ref[rows, pl.ds(d + g * gd, gd)].astype(F32)
            acc_ref[rows, cols] = acc_ref[rows, cols] + gate * yb

    pgd = d // len(POOL_WINDOWS)
    first = pl.program_id(1) == 0
    pos = (pl.program_id(1) * tm + lax.broadcasted_iota(I32, (tm, 1), 0) + 1).astype(F32)
    for gi, win in enumerate(POOL_WINDOWS):
        cols = pl.ds(gi * pgd, pgd)
        xg = p_ref[:, cols]
        halo = jnp.where(first, 0.0, halo_ref[:, cols])
        e = jnp.concatenate([halo, xg], axis=0)
        span = 1
        while span < win:
            e = e + pltpu.roll(e, span, 0)
            span *= 2
        wsum = e[POOL_HALO:]
        dlt = wsum / jnp.minimum(pos, float(win)) - xg
        yc = jnp.dot(dlt.astype(BF16), wp_ref[gi], preferred_element_type=F32) * ps_ref[:, cols]
        gate = gt_ref[:, pl.ds(2 * d + gi * pgd, pgd)].astype(F32)
        o_ref[:, cols] = (acc_ref[:, cols] + gate * yc).astype(o_ref.dtype)


def mixer_merge(z, p, gates, ya, sgu_norm, ws, bs_t, wp, pool_scale):
    b, s, d = p.shape
    tm = min(256, s)
    hb = tm // POOL_HALO
    return pl.pallas_call(
        _mixer_merge_kernel,
        out_shape=jax.ShapeDtypeStruct((b, s, d), BF16),
        grid=(b, s // tm),
        in_specs=[pl.BlockSpec((None, tm, 2 * d), lambda bi, i: (bi, i, 0)),
                  pl.BlockSpec((None, tm, d), lambda bi, i: (bi, i, 0)),
                  pl.BlockSpec((None, POOL_HALO, d), lambda bi, i: (bi, jnp.maximum(i * hb - 1, 0), 0)),
                  pl.BlockSpec((None, tm, 3 * d), lambda bi, i: (bi, i, 0)),
                  pl.BlockSpec((None, tm, d), lambda bi, i: (bi, i, 0)),
                  pl.BlockSpec((1, d), lambda bi, i: (0, 0)),
                  pl.BlockSpec(ws.shape, lambda bi, i: (0, 0, 0)),
                  pl.BlockSpec(bs_t.shape, lambda bi, i: (0, 0)),
                  pl.BlockSpec(wp.shape, lambda bi, i: (0, 0, 0)),
                  pl.BlockSpec((1, d), lambda bi, i: (0, 0))],
        out_specs=pl.BlockSpec((None, tm, d), lambda bi, i: (bi, i, 0)),
        scratch_shapes=[pltpu.VMEM((tm, d), F32)],
        compiler_params=_cparams(("parallel", "arbitrary"), 56),
        name="mixer_merge",
    )(z, p, p, gates, ya, sgu_norm.reshape(1, d), ws, bs_t, wp, pool_scale.reshape(1, d))


def _matmul_res_kernel(a_ref, w_ref, x_ref, g_ref, o_ref):
    y = jnp.dot(a_ref[...], w_ref[...], preferred_element_type=F32)
    o_ref[...] = x_ref[...] + g_ref[...] * y


def matmul_res(a, w, x, gate):
    b, s, d = x.shape
    tm = min(512, s)
    return pl.pallas_call(
        _matmul_res_kernel,
        out_shape=jax.ShapeDtypeStruct((b, s, d), F32),
        grid=(b, s // tm),
        in_specs=[pl.BlockSpec((None, tm, d), lambda bi, i: (bi, i, 0)),
                  pl.BlockSpec((d, d), lambda bi, i: (0, 0)),
                  pl.BlockSpec((None, tm, d), lambda bi, i: (bi, i, 0)),
                  pl.BlockSpec((None, 1, d), lambda bi, i: (bi, 0, 0))],
        out_specs=pl.BlockSpec((None, tm, d), lambda bi, i: (bi, i, 0)),
        compiler_params=_cparams(("parallel", "parallel"), 56),
        name="matmul_res",
    )(a, w, x, gate)


def _pack_bf16_pair(a, b):
    pa = lax.bitcast_convert_type(a.astype(BF16).astype(F32), U32)
    pb = lax.bitcast_convert_type(b.astype(BF16).astype(F32), U32)
    return pa | (pb >> 16)


def _unpack_bf16_pair(w):
    hi = lax.bitcast_convert_type(w & jnp.uint32(0xFFFF0000), F32)
    lo = lax.bitcast_convert_type(w << 16, F32)
    return hi, lo


def _router_kernel(x_ref, g_ref, sc_ref, sh_ref, wr_ref, br_ref,
                   hp_ref, e_ref, w_ref, r_ref, cnt_ref, carry_ref):
    tm, d = x_ref.shape
    half = d // 2

    @pl.when((pl.program_id(0) == 0) & (pl.program_id(1) == 0))
    def _():
        carry_ref[...] = jnp.zeros_like(carry_ref)

    h = _modulated_rmsnorm(x_ref[...], g_ref[...], sc_ref[...], sh_ref[...])
    hp_ref[...] = _pack_bf16_pair(h[:, :half], h[:, half:])

    logits = lax.dot_general(wr_ref[...], h, (((1,), (1,)), ((), ())),
                             preferred_element_type=F32, precision=lax.Precision.HIGHEST) + br_ref[...]
    eio = lax.broadcasted_iota(I32, logits.shape, 0).astype(F32)
    onehot = jnp.zeros(logits.shape, F32)
    vals, sels = [], []
    rem = logits
    for k in range(TOP_K):
        m = jnp.max(rem, axis=0, keepdims=True)
        idx = jnp.min(jnp.where(rem == m, eio, float(N_EXPERTS)), axis=0, keepdims=True)
        sel = eio == idx
        rem = jnp.where(sel, -jnp.inf, rem)
        onehot = onehot + jnp.where(sel, 1.0, 0.0)
        vals.append(m)
        sels.append(sel)
        e_ref[k:k + 1, :] = idx.astype(I32)
    ex = [jnp.exp(v - vals[0]) for v in vals]
    den = ex[0] + ex[1] + ex[2] + ex[3]
    for k in range(TOP_K):
        w_ref[k:k + 1, :] = ex[k] / den

    before = lax.broadcasted_iota(I32, (tm, tm), 0) < lax.broadcasted_iota(I32, (tm, tm), 1)
    cum = jnp.dot(onehot.astype(BF16), jnp.where(before, 1.0, 0.0).astype(BF16),
                  preferred_element_type=F32) + carry_ref[...]
    for k in range(TOP_K):
        r_ref[k:k + 1, :] = jnp.sum(jnp.where(sels[k], cum, 0.0), axis=0, keepdims=True).astype(I32)
    carry_ref[...] = carry_ref[...] + jnp.sum(onehot, axis=1, keepdims=True)
    cnt_ref[...] = jnp.broadcast_to(carry_ref[...], cnt_ref.shape).astype(I32)


def router(x, gain, scale, shift, wr_t, b_router):
    b, s, d = x.shape
    tm = min(512, s)
    nt = s // tm
    t = b * s
    tok = lambda bi, i: (0, bi * nt + i)
    return pl.pallas_call(
        _router_kernel,
        out_shape=(jax.ShapeDtypeStruct((b, s, d // 2), U32),
                   jax.ShapeDtypeStruct((TOP_K, t), I32),
                   jax.ShapeDtypeStruct((TOP_K, t), F32),
                   jax.ShapeDtypeStruct((TOP_K, t), I32),
                   jax.ShapeDtypeStruct((N_EXPERTS, LANES), I32)),
        grid=(b, nt),
        in_specs=[pl.BlockSpec((None, tm, d), lambda bi, i: (bi, i, 0)),
                  pl.BlockSpec((1, d), lambda bi, i: (0, 0)),
                  pl.BlockSpec((None, 1, d), lambda bi, i: (bi, 0, 0)),
                  pl.BlockSpec((None, 1, d), lambda bi, i: (bi, 0, 0)),
                  pl.BlockSpec((N_EXPERTS, d), lambda bi, i: (0, 0)),
                  pl.BlockSpec((N_EXPERTS, 1), lambda bi, i: (0, 0))],
        out_specs=(pl.BlockSpec((None, tm, d // 2), lambda bi, i: (bi, i, 0)),
                   pl.BlockSpec((TOP_K, tm), tok),
                   pl.BlockSpec((TOP_K, tm), tok),
                   pl.BlockSpec((TOP_K, tm), tok),
                   pl.BlockSpec((N_EXPERTS, LANES), lambda bi, i: (0, 0))),
        scratch_shapes=[pltpu.VMEM((N_EXPERTS, 1), F32)],
        compiler_params=_cparams(("arbitrary", "arbitrary"), 40),
        name="router",
    )(x, gain.reshape(1, d), scale, shift, wr_t, b_router.reshape(N_EXPERTS, 1))


def _dest_kernel(ps_ref, e_ref, r_ref, o_ref):
    e = e_ref[...]
    base = jnp.zeros(e.shape, I32)
    for j in range(N_EXPERTS):
        base = jnp.where(e == j, ps_ref[j], base)
    o_ref[...] = base + r_ref[...]


def dest_rows(pad_start, e_t, r_t):
    return pl.pallas_call(
        _dest_kernel,
        out_shape=jax.ShapeDtypeStruct(e_t.shape, I32),
        grid_spec=pltpu.PrefetchScalarGridSpec(
            num_scalar_prefetch=1, grid=(1,),
            in_specs=[pl.BlockSpec(e_t.shape, lambda i, ps: (0, 0)),
                      pl.BlockSpec(e_t.shape, lambda i, ps: (0, 0))],
            out_specs=pl.BlockSpec(e_t.shape, lambda i, ps: (0, 0))),
        compiler_params=_cparams(("arbitrary",), 16),
        name="dest",
    )(pad_start, e_t, r_t)


def _experts_kernel(be_ref, src_first_ref, src_next_ref, slot_prev_ref, slot_last_ref, hp_ref,
                    wg_ref, bg_ref, wu_ref, bu_ref, wd_ref, bd_ref,
                    yt_ref, xbuf, ybuf, gsem, ssem, *, spare_base):
    del be_ref
    i = pl.program_id(0)
    last = pl.num_programs(0) - 1
    cur = lax.rem(i, 2)
    nxt = 1 - cur
    rows = ybuf.shape[1]

    def gather_copy(src, r, buf):
        return pltpu.make_async_copy(hp_ref.at[pl.ds(src, 1)], xbuf.at[buf, pl.ds(r, 1)], gsem.at[buf])

    def scatter_copy(dst, r, buf):
        return pltpu.make_async_copy(ybuf.at[buf, pl.ds(r, 1)], yt_ref.at[pl.ds(dst, 1)], ssem.at[buf])

    def wait_gathers(buf):
        for r in range(rows):
            gather_copy(0, r, buf).wait()

    def wait_scatters(buf):
        for r in range(rows):
            scatter_copy(0, r, buf).wait()

    @pl.when(i == 0)
    def _():
        for r in range(rows):
            gather_copy(src_first_ref[r], r, 0).start()
        ybuf[1] = jnp.zeros(ybuf.shape[1:], ybuf.dtype)

    wait_gathers(cur)

    @pl.when(i >= 1)
    def _():
        wait_scatters(cur)

    hi, lo = _unpack_bf16_pair(xbuf[cur])
    x = jnp.concatenate([hi.astype(BF16), lo.astype(BF16)], axis=1)
    for r in range(rows):
        gather_copy(src_next_ref[r], r, nxt).start()
        dst = jnp.where(i >= 1, slot_prev_ref[r], spare_base + r)
        scatter_copy(dst, r, nxt).start()

    gl = jnp.dot(x, wg_ref[...], preferred_element_type=F32) + bg_ref[...]
    li = jnp.dot(x, wu_ref[...], preferred_element_type=F32) + bu_ref[...]
    gl = jnp.minimum(gl, SWIGLU_LIMIT)
    li = jnp.clip(li, -SWIGLU_LIMIT, SWIGLU_LIMIT)
    act = gl * jax.nn.sigmoid(SWIGLU_ALPHA * gl) * (li + 1.0)
    y = jnp.dot(act.astype(BF16), wd_ref[...], preferred_element_type=F32) + bd_ref[...]
    half = y.shape[1] // 2
    ybuf[cur] = _pack_bf16_pair(y[:, :half], y[:, half:])

    @pl.when(i == last)
    def _():
        for r in range(rows):
            scatter_copy(slot_last_ref[r], r, cur).start()
        wait_scatters(nxt)
        wait_scatters(cur)
        wait_gathers(nxt)


def experts(hp, slots, block_e, wg, bg, wu, bu, wd, bd):
    t, w = hp.shape
    n_rows = slots.shape[0]
    n_blocks = n_rows // MOE_ROWS
    n_e, d, f = wg.shape
    spare_base = t * TOP_K + n_rows
    src = slots % t
    kern = functools.partial(_experts_kernel, spare_base=spare_base)
    smem_block = lambda idx: pl.BlockSpec((MOE_ROWS,), idx, memory_space=pltpu.SMEM)
    return pl.pallas_call(
        kern,
        out_shape=jax.ShapeDtypeStruct((spare_base + MOE_ROWS, w), U32),
        grid_spec=pltpu.PrefetchScalarGridSpec(
            num_scalar_prefetch=1, grid=(n_blocks,),
            in_specs=[smem_block(lambda i, be: (0,)),
                      smem_block(lambda i, be: (jnp.minimum(i + 1, n_blocks - 1),)),
                      smem_block(lambda i, be: (jnp.maximum(i - 1, 0),)),
                      smem_block(lambda i, be: (n_blocks - 1,)),
                      pl.BlockSpec(memory_space=pl.ANY),
                      pl.BlockSpec((None, d, f), lambda i, be: (be[i], 0, 0)),
                      pl.BlockSpec((None, 1, f), lambda i, be: (be[i], 0, 0)),
                      pl.BlockSpec((None, d, f), lambda i, be: (be[i], 0, 0)),
                      pl.BlockSpec((None, 1, f), lambda i, be: (be[i], 0, 0)),
                      pl.BlockSpec((None, f, d), lambda i, be: (be[i], 0, 0)),
                      pl.BlockSpec((None, 1, d), lambda i, be: (be[i], 0, 0))],
            out_specs=pl.BlockSpec(memory_space=pl.ANY),
            scratch_shapes=[pltpu.VMEM((2, MOE_ROWS, w), U32), pltpu.VMEM((2, MOE_ROWS, w), U32),
                            pltpu.SemaphoreType.DMA((2,)), pltpu.SemaphoreType.DMA((2,))]),
        compiler_params=_cparams(("arbitrary",), 56),
        name="experts",
    )(block_e, src, src, slots, slots, hp, wg, bg.reshape(n_e, 1, f), wu, bu.reshape(n_e, 1, f), wd,
      bd.reshape(n_e, 1, d))


def _combine_kernel(w_ref, y0_ref, y1_ref, y2_ref, y3_ref, x_ref, g_ref, fn_ref, o_ref, *, final):
    half = x_ref.shape[1] // 2
    acc_hi = None
    for k, y_ref in enumerate((y0_ref, y1_ref, y2_ref, y3_ref)):
        hi, lo = _unpack_bf16_pair(y_ref[...])
        wk = w_ref[:, k:k + 1]
        if acc_hi is None:
            acc_hi, acc_lo = wk * hi, wk * lo
        else:
            acc_hi, acc_lo = acc_hi + wk * hi, acc_lo + wk * lo
    out = x_ref[...] + g_ref[...] * jnp.concatenate([acc_hi, acc_lo], axis=1)
    if final:
        out = _rms(out, fn_ref[...])
    o_ref[...] = out


def combine(w_tok, yt, x, gate, final_norm, final):
    b, s, d = x.shape
    tm = min(512, s)
    nt = s // tm
    tiles = b * nt
    y_spec = lambda k: pl.BlockSpec((tm, d // 2), lambda bi, i: (k * tiles + bi * nt + i, 0))
    return pl.pallas_call(
        functools.partial(_combine_kernel, final=final),
        out_shape=jax.ShapeDtypeStruct((b, s, d), F32),
        grid=(b, nt),
        in_specs=[pl.BlockSpec((tm, TOP_K), lambda bi, i: (bi * nt + i, 0)),
                  y_spec(0), y_spec(1), y_spec(2), y_spec(3),
                  pl.BlockSpec((None, tm, d), lambda bi, i: (bi, i, 0)),
                  pl.BlockSpec((None, 1, d), lambda bi, i: (bi, 0, 0)),
                  pl.BlockSpec((1, d), lambda bi, i: (0, 0))],
        out_specs=pl.BlockSpec((None, tm, d), lambda bi, i: (bi, i, 0)),
        compiler_params=_cparams(("parallel", "parallel"), 48),
        name="combine",
    )(w_tok, yt, yt, yt, yt, x, gate, final_norm.reshape(1, d))


def _swap_halves(w):
    half = w.shape[-1] // 2
    return jnp.concatenate([w[..., half:], w[..., :half]], axis=-1)


def _rope_tables(s):
    pos = jnp.arange(s, dtype=F32)
    inv_freq = ROPE_THETA ** (-jnp.arange(0, QK_ROPE, 2, dtype=F32) / QK_ROPE)
    ang = pos[:, None] * inv_freq[None, :]
    cos, sin = jnp.cos(ang), jnp.sin(ang)
    zero = jnp.zeros((s, LANES - QK_ROPE), F32)
    return (jnp.concatenate([cos, cos, zero], axis=1), jnp.concatenate([-sin, sin, zero], axis=1))


def kernel(x, c, attn_norm, ffn_norm, w_ada, b_ada, w_in, q_norm, kv_norm, w_uq, w_ukv, sgu_norm, w_s, b_s,
           w_pool, pool_scale, w_o, w_router, b_router, w_gate, b_gate, w_up, b_up, w_down, b_down, final_norm):
    b, s, d = x.shape
    n_layers = w_ada.shape[0]
    t = b * s
    mla_in = Q_LORA + KV_LORA + QK_ROPE
    mods = adaln(c, w_ada, b_ada)
    cc, ss = _rope_tables(s)
    n_blocks = t * TOP_K // MOE_ROWS + N_EXPERTS
    n_rows = n_blocks * MOE_ROWS

    for l in range(n_layers):
        sh1, sc1, g1, sh2, sc2, g2 = [mods[l, :b, i * d:(i + 1) * d].reshape(b, 1, d) for i in range(6)]

        wl = w_in[l]
        k_rope_cols = wl[:, Q_LORA + KV_LORA:mla_in]
        w_lat = jnp.concatenate([wl[:, :mla_in], _swap_halves(k_rope_cols)], axis=1).astype(BF16)
        w_z = wl[:, mla_in:mla_in + 2 * d].astype(BF16)
        w_p = wl[:, mla_in + 2 * d:mla_in + 3 * d].astype(BF16)
        w_g = wl[:, mla_in + 3 * d:].astype(BF16)
        lat = norm_matmul(x, attn_norm[l], sc1, sh1, w_lat, w_lat.shape[1])
        z = norm_matmul(x, attn_norm[l], sc1, sh1, w_z, 1024, "gelu", BF16)
        p = norm_matmul(x, attn_norm[l], sc1, sh1, w_p, 1024)
        gates = norm_matmul(x, attn_norm[l], sc1, sh1, w_g, 1024, "sigmoid", BF16)

        wq = w_uq[l]
        wq = jnp.concatenate([wq, _swap_halves(wq[..., QK_NOPE:])], axis=-1)
        wq = wq.transpose(1, 0, 2).astype(BF16)
        wkv = w_ukv[l].transpose(1, 0, 2).astype(BF16)
        q, k, v = mla_prep(lat, q_norm[l], kv_norm[l], cc, ss, wq, wkv)
        ya = flash(q, k, v)

        merged = mixer_merge(z, p, gates, ya, sgu_norm[l], w_s[l].astype(BF16), b_s[l].T,
                             w_pool[l].astype(BF16), pool_scale[l])
        x = matmul_res(merged, w_o[l].astype(BF16), x, g1)

        hp, e_t, w_t, r_t, cnt = router(x, ffn_norm[l], sc2, sh2, w_router[l].T, b_router[l])
        counts = cnt[:, 0]
        padded = ((counts + MOE_ROWS - 1) // MOE_ROWS) * MOE_ROWS
        pad_end = jnp.cumsum(padded)
        pad_start = pad_end - padded
        block_start = jnp.arange(n_blocks, dtype=I32) * MOE_ROWS
        block_e = jnp.minimum(jnp.sum((pad_end[None, :] <= block_start[:, None]).astype(I32), axis=1),
                              N_EXPERTS - 1)
        dest_flat = dest_rows(pad_start.astype(I32), e_t, r_t).reshape(TOP_K * t)
        slots = (t * TOP_K + jnp.arange(n_rows, dtype=I32)).at[dest_flat].set(
            jnp.arange(t * TOP_K, dtype=I32), unique_indices=True)
        yt = experts(hp.reshape(t, d // 2), slots, block_e, w_gate[l].astype(BF16), b_gate[l],
                     w_up[l].astype(BF16), b_up[l], w_down[l].astype(BF16), b_down[l])
        x = combine(w_t.T, yt, x, g2, final_norm, final=(l == n_layers - 1))
    return x
```

```python
import functools
import math

import jax
import jax.numpy as jnp
from jax import lax
from jax.experimental import pallas as pl
from jax.experimental.pallas import tpu as pltpu

F32 = jnp.float32
BF16 = jnp.bfloat16
U32 = jnp.uint32
I32 = jnp.int32

N_HEADS = 16
Q_LORA = 512
KV_LORA = 512
QK_NOPE = 128
QK_ROPE = 64
V_DIM = 128
ROPE_THETA = 10000.0
SGU_CHUNK = 128
SGU_GROUPS = 16
POOL_WINDOWS = (2, 4, 8, 16)
POOL_HALO = 16
N_EXPERTS = 32
TOP_K = 4
SWIGLU_LIMIT = 7.0
SWIGLU_ALPHA = 1.702
EPS = 1e-6
LANES = 128
MOE_ROWS = 256
FLASH_TQ = 2048
FLASH_TK = 2048
FLASH_TK_DIAG = 1024
FLASH_SUB = 1024
MIB = 1024 * 1024


def _cparams(semantics, vmem_mib):
    return pltpu.CompilerParams(dimension_semantics=semantics, vmem_limit_bytes=vmem_mib * MIB)


def _adaln_kernel(c_ref, w_ref, b_ref, o_ref):
    c = c_ref[...]
    a = c * jax.nn.sigmoid(c)
    o_ref[...] = jnp.dot(a, w_ref[...], preferred_element_type=F32,
                         precision=lax.Precision.HIGHEST) + b_ref[...]


def adaln(c, w_ada, b_ada):
    n_layers, d, n = w_ada.shape
    rows = 8
    cp = jnp.zeros((rows, d), F32).at[:c.shape[0]].set(c)
    tn = 1024
    return pl.pallas_call(
        _adaln_kernel,
        out_shape=jax.ShapeDtypeStruct((n_layers, rows, n), F32),
        grid=(n_layers, n // tn),
        in_specs=[pl.BlockSpec((rows, d), lambda l, j: (0, 0)),
                  pl.BlockSpec((None, d, tn), lambda l, j: (l, 0, j)),
                  pl.BlockSpec((None, 1, tn), lambda l, j: (l, 0, j))],
        out_specs=pl.BlockSpec((None, rows, tn), lambda l, j: (l, 0, j)),
        compiler_params=_cparams(("parallel", "parallel"), 40),
        name="adaln",
    )(cp, w_ada, b_ada.reshape(n_layers, 1, n))


def _modulated_rmsnorm(x, gain, scale, shift):
    y = x * lax.rsqrt(jnp.mean(x * x, axis=-1, keepdims=True) + EPS) * gain
    return y * (1.0 + scale) + shift


def _gelu_tanh(x):
    return 0.5 * x * (1.0 + jnp.tanh(math.sqrt(2.0 / math.pi) * (x + 0.044715 * (x * x * x))))


_EPILOGUES = {None: lambda y: y, "gelu": _gelu_tanh, "sigmoid": jax.nn.sigmoid}


def _mod_norm_kernel(x_ref, g_ref, sc_ref, sh_ref, o_ref):
    o_ref[...] = _modulated_rmsnorm(x_ref[...], g_ref[...], sc_ref[...], sh_ref[...]).astype(o_ref.dtype)


def mod_norm(x, gain, scale, shift):
    b, s, d = x.shape
    tm = min(512, s)
    return pl.pallas_call(
        _mod_norm_kernel,
        out_shape=jax.ShapeDtypeStruct((b, s, d), BF16),
        grid=(b, s // tm),
        in_specs=[pl.BlockSpec((None, tm, d), lambda bi, i: (bi, i, 0)),
                  pl.BlockSpec((1, d), lambda bi, i: (0, 0)),
                  pl.BlockSpec((None, 1, d), lambda bi, i: (bi, 0, 0)),
                  pl.BlockSpec((None, 1, d), lambda bi, i: (bi, 0, 0))],
        out_specs=pl.BlockSpec((None, tm, d), lambda bi, i: (bi, i, 0)),
        compiler_params=_cparams(("parallel", "parallel"), 32),
        name="mod_norm",
    )(x, gain.reshape(1, d), scale, shift)


def _proj_kernel(h_ref, w_ref, o_ref, *, epilogue):
    y = jnp.dot(h_ref[...], w_ref[...], preferred_element_type=F32)
    o_ref[...] = _EPILOGUES[epilogue](y).astype(o_ref.dtype)


def proj(h, w, tn, epilogue=None, out_dtype=F32):
    b, s, d = h.shape
    n = w.shape[1]
    tm = min(1024, s)
    return pl.pallas_call(
        functools.partial(_proj_kernel, epilogue=epilogue),
        out_shape=jax.ShapeDtypeStruct((b, s, n), out_dtype),
        grid=(b, s // tm, n // tn),
        in_specs=[pl.BlockSpec((None, tm, d), lambda bi, i, j: (bi, i, 0)),
                  pl.BlockSpec((d, tn), lambda bi, i, j: (0, j))],
        out_specs=pl.BlockSpec((None, tm, tn), lambda bi, i, j: (bi, i, j)),
        compiler_params=_cparams(("parallel", "parallel", "parallel"), 48),
        name="proj",
    )(h, w)


def _rms(x, gain):
    return x * lax.rsqrt(jnp.mean(x * x, axis=-1, keepdims=True) + EPS) * gain


def _rope_tile(t, cc, ss):
    return t * cc + pltpu.roll(t, LANES // 2, 1) * ss


def _mla_prep_kernel(lat_ref, qn_ref, kvn_ref, cc_ref, ss_ref, wq_ref, wkv_ref, q_ref, k_ref, v_ref):
    cc = cc_ref[...]
    ss = ss_ref[...]
    cq = _rms(lat_ref[:, :Q_LORA], qn_ref[...]).astype(BF16)
    ckv = _rms(lat_ref[:, Q_LORA:Q_LORA + KV_LORA], kvn_ref[...]).astype(BF16)
    kpe = _rope_tile(lat_ref[:, Q_LORA + KV_LORA:], cc, ss).astype(BF16)
    scale = (QK_NOPE + QK_ROPE) ** -0.5 * math.log2(math.e)
    for h in range(N_HEADS):
        qres = jnp.dot(cq, wq_ref[h], preferred_element_type=F32)
        q_ref[h, :, :QK_NOPE] = (qres[:, :QK_NOPE] * scale).astype(BF16)
        q_ref[h, :, QK_NOPE:] = (_rope_tile(qres[:, QK_NOPE:], cc, ss) * scale).astype(BF16)
        kvres = jnp.dot(ckv, wkv_ref[h], preferred_element_type=F32)
        k_ref[h, :, :QK_NOPE] = kvres[:, :QK_NOPE].astype(BF16)
        k_ref[h, :, QK_NOPE:] = kpe
        v_ref[h] = kvres[:, QK_NOPE:].astype(BF16)


def mla_prep(lat, q_norm, kv_norm, cc, ss, wq, wkv):
    b, s, nlat = lat.shape
    tm = min(512, s)
    hd = QK_NOPE + LANES
    return pl.pallas_call(
        _mla_prep_kernel,
        out_shape=(jax.ShapeDtypeStruct((b, N_HEADS, s, hd), BF16),
                   jax.ShapeDtypeStruct((b, N_HEADS, s, hd), BF16),
                   jax.ShapeDtypeStruct((b, N_HEADS, s, V_DIM), BF16)),
        grid=(b, s // tm),
        in_specs=[pl.BlockSpec((None, tm, nlat), lambda bi, i: (bi, i, 0)),
                  pl.BlockSpec((1, Q_LORA), lambda bi, i: (0, 0)),
                  pl.BlockSpec((1, KV_LORA), lambda bi, i: (0, 0)),
                  pl.BlockSpec((tm, LANES), lambda bi, i: (i, 0)),
                  pl.BlockSpec((tm, LANES), lambda bi, i: (i, 0)),
                  pl.BlockSpec((N_HEADS, Q_LORA, hd), lambda bi, i: (0, 0, 0)),
                  pl.BlockSpec((N_HEADS, KV_LORA, QK_NOPE + V_DIM), lambda bi, i: (0, 0, 0))],
        out_specs=(pl.BlockSpec((None, N_HEADS, tm, hd), lambda bi, i: (bi, 0, i, 0)),
                   pl.BlockSpec((None, N_HEADS, tm, hd), lambda bi, i: (bi, 0, i, 0)),
                   pl.BlockSpec((None, N_HEADS, tm, V_DIM), lambda bi, i: (bi, 0, i, 0))),
        compiler_params=_cparams(("parallel", "parallel"), 56),
        name="mla_prep",
    )(lat, q_norm.reshape(1, -1), kv_norm.reshape(1, -1), cc, ss, wq, wkv)


def _flash_kernel(q_ref, k_ref, v_ref, o_ref, m_ref, l_ref, acc_ref, *, tk, tkd, sub):
    qi = pl.program_id(2)
    tq = q_ref.shape[0]
    sub_tiles = tq // sub
    m_ref[...] = jnp.full_like(m_ref, -jnp.inf)
    l_ref[...] = jnp.zeros_like(l_ref)
    acc_ref[...] = jnp.zeros_like(acc_ref)

    def attend(q_rows, kv_rows, diag_shift):
        s = lax.dot_general(q_ref[q_rows, :], k_ref[kv_rows, :], (((1,), (1,)), ((), ())),
                            preferred_element_type=F32)
        if diag_shift is not None:
            row = lax.broadcasted_iota(I32, s.shape, 0)
            col = lax.broadcasted_iota(I32, s.shape, 1) + diag_shift
            s = jnp.where(col <= row, s, -1e30)
        m_prev = m_ref[q_rows, :]
        m_next = jnp.maximum(m_prev, jnp.max(s, axis=-1, keepdims=True))
        alpha = jnp.exp2(m_prev - m_next)
        ps = [jnp.exp2(s[:, c * LANES:(c + 1) * LANES] - m_next) for c in range(s.shape[1] // LANES)]
        psum = ps[0]
        for pc in ps[1:]:
            psum = psum + pc
        l_ref[q_rows, :] = alpha * l_ref[q_rows, :] + psum
        p = jnp.concatenate(ps, axis=1).astype(BF16)
        acc_ref[q_rows, :] = alpha * acc_ref[q_rows, :] + jnp.dot(p, v_ref[kv_rows, :],
                                                                  preferred_element_type=F32)
        m_ref[q_rows, :] = m_next

    def body(j, carry):
        kv_rows = pl.ds(pl.multiple_of(j * tk, tk), tk)
        for t in range(sub_tiles):
            attend(pl.ds(t * sub, sub), kv_rows, None)
        return carry

    lax.fori_loop(0, qi * (tq // tk), body, 0)
    for c in range(tq // tkd):
        kv_rows = pl.ds(pl.multiple_of(qi * tq + c * tkd, tkd), tkd)
        for t in range(sub_tiles):
            first_row, last_row = t * sub, (t + 1) * sub - 1
            first_col, last_col = c * tkd, (c + 1) * tkd - 1
            if first_col > last_row:
                continue
            visible = last_col <= first_row
            attend(pl.ds(t * sub, sub), kv_rows, None if visible else first_col - first_row)
    o_ref[...] = (acc_ref[...] / jnp.sum(l_ref[...], axis=-1, keepdims=True)).astype(o_ref.dtype)


def flash(q, k, v):
    b, h, s, hd = q.shape
    tq = min(FLASH_TQ, s)
    tk = min(FLASH_TK, tq)
    sub = min(FLASH_SUB, tq)
    tkd = min(FLASH_TK_DIAG, tq)
    return pl.pallas_call(
        functools.partial(_flash_kernel, tk=tk, tkd=tkd, sub=sub),
        out_shape=jax.ShapeDtypeStruct((b, s, h * V_DIM), BF16),
        grid=(b, h, s // tq),
        in_specs=[pl.BlockSpec((None, None, tq, hd), lambda bi, hi, qi: (bi, hi, qi, 0)),
                  pl.BlockSpec((None, None, s, hd), lambda bi, hi, qi: (bi, hi, 0, 0)),
                  pl.BlockSpec((None, None, s, V_DIM), lambda bi, hi, qi: (bi, hi, 0, 0))],
        out_specs=pl.BlockSpec((None, tq, V_DIM), lambda bi, hi, qi: (bi, qi, hi)),
        scratch_shapes=[pltpu.VMEM((tq, LANES), F32), pltpu.VMEM((tq, LANES), F32),
                        pltpu.VMEM((tq, V_DIM), F32)],
        compiler_params=_cparams(("parallel", "parallel", "arbitrary"), 48),
        name="flash",
    )(q, k, v)


def _mixer_merge_kernel(z_ref, p_ref, halo_ref, gt_ref, ya_ref, sn_ref, ws_ref, bs_ref,
                        wp_ref, ps_ref, o_ref, acc_ref):
    tm, d = o_ref.shape
    gd = d // SGU_GROUPS
    tri_r = lax.broadcasted_iota(I32, (SGU_CHUNK, SGU_CHUNK), 0)
    tri_c = lax.broadcasted_iota(I32, (SGU_CHUNK, SGU_CHUNK), 1)
    tril = tri_c <= tri_r

    acc_ref[...] = gt_ref[:, :d].astype(F32) * ya_ref[...].astype(F32)

    for c in range(tm // SGU_CHUNK):
        rows = pl.ds(c * SGU_CHUNK, SGU_CHUNK)
        v = z_ref[rows, d:].astype(F32)
        mu = jnp.mean(v, axis=-1, keepdims=True)
        vc = v - mu
        var = jnp.mean(vc * vc, axis=-1, keepdims=True)
        vn = (vc * lax.rsqrt(var + EPS) * sn_ref[...]).astype(BF16)
        for g in range(SGU_GROUPS):
            cols = pl.ds(g * gd, gd)
            w = jnp.where(tril, ws_ref[g], jnp.zeros((), BF16))
            mixed = jnp.dot(w, vn[:, g * gd:(g + 1) * gd], preferred_element_type=F32) + bs_ref[:, g:g + 1]
            yb = z_ref[rows, cols].astype(F32) * mixed
            gate = gt_ref[rows, pl.ds(d + g * gd, gd)].astype(F32)
            acc_ref[rows, cols] = acc_ref[rows, cols] + gate * yb

    pgd = d // len(POOL_WINDOWS)
    first = pl.program_id(1) == 0
    pos = (pl.program_id(1) * tm + lax.broadcasted_iota(I32, (tm, 1), 0) + 1).astype(F32)
    for gi, win in enumerate(POOL_WINDOWS):
        cols = pl.ds(gi * pgd, pgd)
        xg = p_ref[:, cols]
        halo = jnp.where(first, 0.0, halo_ref[:, cols])
        e = jnp.concatenate([halo, xg], axis=0)
        span = 1
        while span < win:
            e = e + pltpu.roll(e, span, 0)
            span *= 2
        wsum = e[POOL_HALO:]
        dlt = wsum / jnp.minimum(pos, float(win)) - xg
        yc = jnp.dot(dlt.astype(BF16), wp_ref[gi], preferred_element_type=F32) * ps_ref[:, cols]
        gate = gt_ref[:, pl.ds(2 * d + gi * pgd, pgd)].astype(F32)
        o_ref[:, cols] = (acc_ref[:, cols] + gate * yc).astype(o_ref.dtype)


def mixer_merge(z, p, gates, ya, sgu_norm, ws, bs_t, wp, pool_scale):
    b, s, d = p.shape
    tm = min(256, s)
    hb = tm // POOL_HALO
    return pl.pallas_call(
        _mixer_merge_kernel,
        out_shape=jax.ShapeDtypeStruct((b, s, d), BF16),
        grid=(b, s // tm),
        in_specs=[pl.BlockSpec((None, tm, 2 * d), lambda bi, i: (bi, i, 0)),
                  pl.BlockSpec((None, tm, d), lambda bi, i: (bi, i, 0)),
                  pl.BlockSpec((None, POOL_HALO, d), lambda bi, i: (bi, jnp.maximum(i * hb - 1, 0), 0)),
                  pl.BlockSpec((None, tm, 3 * d), lambda bi, i: (bi, i, 0)),
                  pl.BlockSpec((None, tm, d), lambda bi, i: (bi, i, 0)),
                  pl.BlockSpec((1, d), lambda bi, i: (0, 0)),
                  pl.BlockSpec(ws.shape, lambda bi, i: (0, 0, 0)),
                  pl.BlockSpec(bs_t.shape, lambda bi, i: (0, 0)),
                  pl.BlockSpec(wp.shape, lambda bi, i: (0, 0, 0)),
                  pl.BlockSpec((1, d), lambda bi, i: (0, 0))],
        out_specs=pl.BlockSpec((None, tm, d), lambda bi, i: (bi, i, 0)),
        scratch_shapes=[pltpu.VMEM((tm, d), F32)],
        compiler_params=_cparams(("parallel", "arbitrary"), 56),
        name="mixer_merge",
    )(z, p, p, gates, ya, sgu_norm.reshape(1, d), ws, bs_t, wp, pool_scale.reshape(1, d))


def _matmul_res_kernel(a_ref, w_ref, x_ref, g_ref, o_ref):
    y = jnp.dot(a_ref[...], w_ref[...], preferred_element_type=F32)
    o_ref[...] = x_ref[...] + g_ref[...] * y


def matmul_res(a, w, x, gate):
    b, s, d = x.shape
    tm = min(512, s)
    return pl.pallas_call(
        _matmul_res_kernel,
        out_shape=jax.ShapeDtypeStruct((b, s, d), F32),
        grid=(b, s // tm),
        in_specs=[pl.BlockSpec((None, tm, d), lambda bi, i: (bi, i, 0)),
                  pl.BlockSpec((d, d), lambda bi, i: (0, 0)),
                  pl.BlockSpec((None, tm, d), lambda bi, i: (bi, i, 0)),
                  pl.BlockSpec((None, 1, d), lambda bi, i: (bi, 0, 0))],
        out_specs=pl.BlockSpec((None, tm, d), lambda bi, i: (bi, i, 0)),
        compiler_params=_cparams(("parallel", "parallel"), 56),
        name="matmul_res",
    )(a, w, x, gate)


def _pack_bf16_pair(a, b):
    pa = lax.bitcast_convert_type(a.astype(BF16).astype(F32), U32)
    pb = lax.bitcast_convert_type(b.astype(BF16).astype(F32), U32)
    return pa | (pb >> 16)


def _unpack_bf16_pair(w):
    hi = lax.bitcast_convert_type(w & jnp.uint32(0xFFFF0000), F32)
    lo = lax.bitcast_convert_type(w << 16, F32)
    return hi, lo


def _router_kernel(x_ref, g_ref, sc_ref, sh_ref, wr_ref, br_ref,
                   hp_ref, e_ref, w_ref, r_ref, cnt_ref, carry_ref):
    tm, d = x_ref.shape
    half = d // 2

    @pl.when((pl.program_id(0) == 0) & (pl.program_id(1) == 0))
    def _():
        carry_ref[...] = jnp.zeros_like(carry_ref)

    h = _modulated_rmsnorm(x_ref[...], g_ref[...], sc_ref[...], sh_ref[...])
    hp_ref[...] = _pack_bf16_pair(h[:, :half], h[:, half:])

    logits = lax.dot_general(wr_ref[...], h, (((1,), (1,)), ((), ())),
                             preferred_element_type=F32, precision=lax.Precision.HIGHEST) + br_ref[...]
    eio = lax.broadcasted_iota(I32, logits.shape, 0).astype(F32)
    onehot = jnp.zeros(logits.shape, F32)
    vals, sels = [], []
    rem = logits
    for k in range(TOP_K):
        m = jnp.max(rem, axis=0, keepdims=True)
        idx = jnp.min(jnp.where(rem == m, eio, float(N_EXPERTS)), axis=0, keepdims=True)
        sel = eio == idx
        rem = jnp.where(sel, -jnp.inf, rem)
        onehot = onehot + jnp.where(sel, 1.0, 0.0)
        vals.append(m)
        sels.append(sel)
        e_ref[k:k + 1, :] = idx.astype(I32)
    ex = [jnp.exp(v - vals[0]) for v in vals]
    den = ex[0] + ex[1] + ex[2] + ex[3]
    for k in range(TOP_K):
        w_ref[k:k + 1, :] = ex[k] / den

    before = lax.broadcasted_iota(I32, (tm, tm), 0) < lax.broadcasted_iota(I32, (tm, tm), 1)
    cum = jnp.dot(onehot.astype(BF16), jnp.where(before, 1.0, 0.0).astype(BF16),
                  preferred_element_type=F32) + carry_ref[...]
    for k in range(TOP_K):
        r_ref[k:k + 1, :] = jnp.sum(jnp.where(sels[k], cum, 0.0), axis=0, keepdims=True).astype(I32)
    carry_ref[...] = carry_ref[...] + jnp.sum(onehot, axis=1, keepdims=True)
    cnt_ref[...] = jnp.broadcast_to(carry_ref[...], cnt_ref.shape).astype(I32)


def router(x, gain, scale, shift, wr_t, b_router):
    b, s, d = x.shape
    tm = min(512, s)
    nt = s // tm
    t = b * s
    tok = lambda bi, i: (0, bi * nt + i)
    return pl.pallas_call(
        _router_kernel,
        out_shape=(jax.ShapeDtypeStruct((b, s, d // 2), U32),
                   jax.ShapeDtypeStruct((TOP_K, t), I32),
                   jax.ShapeDtypeStruct((TOP_K, t), F32),
                   jax.ShapeDtypeStruct((TOP_K, t), I32),
                   jax.ShapeDtypeStruct((N_EXPERTS, LANES), I32)),
        grid=(b, nt),
        in_specs=[pl.BlockSpec((None, tm, d), lambda bi, i: (bi, i, 0)),
                  pl.BlockSpec((1, d), lambda bi, i: (0, 0)),
                  pl.BlockSpec((None, 1, d), lambda bi, i: (bi, 0, 0)),
                  pl.BlockSpec((None, 1, d), lambda bi, i: (bi, 0, 0)),
                  pl.BlockSpec((N_EXPERTS, d), lambda bi, i: (0, 0)),
                  pl.BlockSpec((N_EXPERTS, 1), lambda bi, i: (0, 0))],
        out_specs=(pl.BlockSpec((None, tm, d // 2), lambda bi, i: (bi, i, 0)),
                   pl.BlockSpec((TOP_K, tm), tok),
                   pl.BlockSpec((TOP_K, tm), tok),
                   pl.BlockSpec((TOP_K, tm), tok),
                   pl.BlockSpec((N_EXPERTS, LANES), lambda bi, i: (0, 0))),
        scratch_shapes=[pltpu.VMEM((N_EXPERTS, 1), F32)],
        compiler_params=_cparams(("arbitrary", "arbitrary"), 40),
        name="router",
    )(x, gain.reshape(1, d), scale, shift, wr_t, b_router.reshape(N_EXPERTS, 1))


def _dest_kernel(ps_ref, e_ref, r_ref, o_ref):
    e = e_ref[...]
    base = jnp.zeros(e.shape, I32)
    for j in range(N_EXPERTS):
        base = jnp.where(e == j, ps_ref[j], base)
    o_ref[...] = base + r_ref[...]


def dest_rows(pad_start, e_t, r_t):
    return pl.pallas_call(
        _dest_kernel,
        out_shape=jax.ShapeDtypeStruct(e_t.shape, I32),
        grid_spec=pltpu.PrefetchScalarGridSpec(
            num_scalar_prefetch=1, grid=(1,),
            in_specs=[pl.BlockSpec(e_t.shape, lambda i, ps: (0, 0)),
                      pl.BlockSpec(e_t.shape, lambda i, ps: (0, 0))],
            out_specs=pl.BlockSpec(e_t.shape, lambda i, ps: (0, 0))),
        compiler_params=_cparams(("arbitrary",), 16),
        name="dest",
    )(pad_start, e_t, r_t)


def _experts_kernel(be_ref, src_first_ref, src_next_ref, slot_prev_ref, slot_last_ref, hp_ref,
                    wg_ref, bg_ref, wu_ref, bu_ref, wd_ref, bd_ref,
                    yt_ref, xbuf, ybuf, gsem, ssem, *, spare_base):
    del be_ref
    i = pl.program_id(0)
    last = pl.num_programs(0) - 1
    cur = lax.rem(i, 2)
    nxt = 1 - cur
    rows = ybuf.shape[1]

    def gather_copy(src, r, buf):
        return pltpu.make_async_copy(hp_ref.at[pl.ds(src, 1)], xbuf.at[buf, pl.ds(r, 1)], gsem.at[buf])

    def scatter_copy(dst, r, buf):
        return pltpu.make_async_copy(ybuf.at[buf, pl.ds(r, 1)], yt_ref.at[pl.ds(dst, 1)], ssem.at[buf])

    def wait_gathers(buf):
        for r in range(rows):
            gather_copy(0, r, buf).wait()

    def wait_scatters(buf):
        for r in range(rows):
            scatter_copy(0, r, buf).wait()

    @pl.when(i == 0)
    def _():
        for r in range(rows):
            gather_copy(src_first_ref[r], r, 0).start()
        ybuf[1] = jnp.zeros(ybuf.shape[1:], ybuf.dtype)

    wait_gathers(cur)

    @pl.when(i >= 1)
    def _():
        wait_scatters(cur)

    hi, lo = _unpack_bf16_pair(xbuf[cur])
    x = jnp.concatenate([hi.astype(BF16), lo.astype(BF16)], axis=1)
    for r in range(rows):
        gather_copy(src_next_ref[r], r, nxt).start()
        dst = jnp.where(i >= 1, slot_prev_ref[r], spare_base + r)
        scatter_copy(dst, r, nxt).start()

    gl = jnp.dot(x, wg_ref[...], preferred_element_type=F32) + bg_ref[...]
    li = jnp.dot(x, wu_ref[...], preferred_element_type=F32) + bu_ref[...]
    gl = jnp.minimum(gl, SWIGLU_LIMIT)
    li = jnp.clip(li, -SWIGLU_LIMIT, SWIGLU_LIMIT)
    act = gl * jax.nn.sigmoid(SWIGLU_ALPHA * gl) * (li + 1.0)
    y = jnp.dot(act.astype(BF16), wd_ref[...], preferred_element_type=F32) + bd_ref[...]
    half = y.shape[1] // 2
    ybuf[cur] = _pack_bf16_pair(y[:, :half], y[:, half:])

    @pl.when(i == last)
    def _():
        for r in range(rows):
            scatter_copy(slot_last_ref[r], r, cur).start()
        wait_scatters(nxt)
        wait_scatters(cur)
        wait_gathers(nxt)


def experts(hp, slots, block_e, wg, bg, wu, bu, wd, bd):
    t, w = hp.shape
    n_rows = slots.shape[0]
    n_blocks = n_rows // MOE_ROWS
    n_e, d, f = wg.shape
    spare_base = t * TOP_K + n_rows
    src = slots % t
    kern = functools.partial(_experts_kernel, spare_base=spare_base)
    smem_block = lambda idx: pl.BlockSpec((MOE_ROWS,), idx, memory_space=pltpu.SMEM)
    return pl.pallas_call(
        kern,
        out_shape=jax.ShapeDtypeStruct((spare_base + MOE_ROWS, w), U32),
        grid_spec=pltpu.PrefetchScalarGridSpec(
            num_scalar_prefetch=1, grid=(n_blocks,),
            in_specs=[smem_block(lambda i, be: (0,)),
                      smem_block(lambda i, be: (jnp.minimum(i + 1, n_blocks - 1),)),
                      smem_block(lambda i, be: (jnp.maximum(i - 1, 0),)),
                      smem_block(lambda i, be: (n_blocks - 1,)),
                      pl.BlockSpec(memory_space=pl.ANY),
                      pl.BlockSpec((None, d, f), lambda i, be: (be[i], 0, 0)),
                      pl.BlockSpec((None, 1, f), lambda i, be: (be[i], 0, 0)),
                      pl.BlockSpec((None, d, f), lambda i, be: (be[i], 0, 0)),
                      pl.BlockSpec((None, 1, f), lambda i, be: (be[i], 0, 0)),
                      pl.BlockSpec((None, f, d), lambda i, be: (be[i], 0, 0)),
                      pl.BlockSpec((None, 1, d), lambda i, be: (be[i], 0, 0))],
            out_specs=pl.BlockSpec(memory_space=pl.ANY),
            scratch_shapes=[pltpu.VMEM((2, MOE_ROWS, w), U32), pltpu.VMEM((2, MOE_ROWS, w), U32),
                            pltpu.SemaphoreType.DMA((2,)), pltpu.SemaphoreType.DMA((2,))]),
        compiler_params=_cparams(("arbitrary",), 56),
        name="experts",
    )(block_e, src, src, slots, slots, hp, wg, bg.reshape(n_e, 1, f), wu, bu.reshape(n_e, 1, f), wd,
      bd.reshape(n_e, 1, d))


def _combine_kernel(w_ref, y0_ref, y1_ref, y2_ref, y3_ref, x_ref, g_ref, fn_ref, o_ref, *, final):
    half = x_ref.shape[1] // 2
    acc_hi = None
    for k, y_ref in enumerate((y0_ref, y1_ref, y2_ref, y3_ref)):
        hi, lo = _unpack_bf16_pair(y_ref[...])
        wk = w_ref[:, k:k + 1]
        if acc_hi is None:
            acc_hi, acc_lo = wk * hi, wk * lo
        else:
            acc_hi, acc_lo = acc_hi + wk * hi, acc_lo + wk * lo
    out = x_ref[...] + g_ref[...] * jnp.concatenate([acc_hi, acc_lo], axis=1)
    if final:
        out = _rms(out, fn_ref[...])
    o_ref[...] = out


def combine(w_tok, yt, x, gate, final_norm, final):
    b, s, d = x.shape
    tm = min(512, s)
    nt = s // tm
    tiles = b * nt
    y_spec = lambda k: pl.BlockSpec((tm, d // 2), lambda bi, i: (k * tiles + bi * nt + i, 0))
    return pl.pallas_call(
        functools.partial(_combine_kernel, final=final),
        out_shape=jax.ShapeDtypeStruct((b, s, d), F32),
        grid=(b, nt),
        in_specs=[pl.BlockSpec((tm, TOP_K), lambda bi, i: (bi * nt + i, 0)),
                  y_spec(0), y_spec(1), y_spec(2), y_spec(3),
                  pl.BlockSpec((None, tm, d), lambda bi, i: (bi, i, 0)),
                  pl.BlockSpec((None, 1, d), lambda bi, i: (bi, 0, 0)),
                  pl.BlockSpec((1, d), lambda bi, i: (0, 0))],
        out_specs=pl.BlockSpec((None, tm, d), lambda bi, i: (bi, i, 0)),
        compiler_params=_cparams(("parallel", "parallel"), 48),
        name="combine",
    )(w_tok, yt, yt, yt, yt, x, gate, final_norm.reshape(1, d))


def _swap_halves(w):
    half = w.shape[-1] // 2
    return jnp.concatenate([w[..., half:], w[..., :half]], axis=-1)


def _rope_tables(s):
    pos = jnp.arange(s, dtype=F32)
    inv_freq = ROPE_THETA ** (-jnp.arange(0, QK_ROPE, 2, dtype=F32) / QK_ROPE)
    ang = pos[:, None] * inv_freq[None, :]
    cos, sin = jnp.cos(ang), jnp.sin(ang)
    zero = jnp.zeros((s, LANES - QK_ROPE), F32)
    return (jnp.concatenate([cos, cos, zero], axis=1), jnp.concatenate([-sin, sin, zero], axis=1))


def kernel(x, c, attn_norm, ffn_norm, w_ada, b_ada, w_in, q_norm, kv_norm, w_uq, w_ukv, sgu_norm, w_s, b_s,
           w_pool, pool_scale, w_o, w_router, b_router, w_gate, b_gate, w_up, b_up, w_down, b_down, final_norm):
    b, s, d = x.shape
    n_layers = w_ada.shape[0]
    t = b * s
    mla_in = Q_LORA + KV_LORA + QK_ROPE
    mods = adaln(c, w_ada, b_ada)
    cc, ss = _rope_tables(s)
    n_blocks = t * TOP_K // MOE_ROWS + N_EXPERTS
    n_rows = n_blocks * MOE_ROWS

    for l in range(n_layers):
        sh1, sc1, g1, sh2, sc2, g2 = [mods[l, :b, i * d:(i + 1) * d].reshape(b, 1, d) for i in range(6)]

        wl = w_in[l]
        k_rope_cols = wl[:, Q_LORA + KV_LORA:mla_in]
        w_lat = jnp.concatenate([wl[:, :mla_in], _swap_halves(k_rope_cols)], axis=1).astype(BF16)
        w_z = wl[:, mla_in:mla_in + 2 * d].astype(BF16)
        w_p = wl[:, mla_in + 2 * d:mla_in + 3 * d].astype(BF16)
        w_g = wl[:, mla_in + 3 * d:].astype(BF16)
        h = mod_norm(x, attn_norm[l], sc1, sh1)
        lat = proj(h, w_lat, w_lat.shape[1])
        z = proj(h, w_z, 1024, "gelu", BF16)
        p = proj(h, w_p, 1024)
        gates = proj(h, w_g, 1024, "sigmoid", BF16)

        wq = w_uq[l]
        wq = jnp.concatenate([wq, _swap_halves(wq[..., QK_NOPE:])], axis=-1)
        wq = wq.transpose(1, 0, 2).astype(BF16)
        wkv = w_ukv[l].transpose(1, 0, 2).astype(BF16)
        q, k, v = mla_prep(lat, q_norm[l], kv_norm[l], cc, ss, wq, wkv)
        ya = flash(q, k, v)

        merged = mixer_merge(z, p, gates, ya, sgu_norm[l], w_s[l].astype(BF16), b_s[l].T,
                             w_pool[l].astype(BF16), pool_scale[l])
        x = matmul_res(merged, w_o[l].astype(BF16), x, g1)

        hp, e_t, w_t, r_t, cnt = router(x, ffn_norm[l], sc2, sh2, w_router[l].T, b_router[l])
        counts = cnt[:, 0]
        padded = ((counts + MOE_ROWS - 1) // MOE_ROWS) * MOE_ROWS
        pad_end = jnp.cumsum(padded)
        pad_start = pad_end - padded
        block_start = jnp.arange(n_blocks, dtype=I32) * MOE_ROWS
        block_e = jnp.minimum(jnp.sum((pad_end[None, :] <= block_start[:, None]).astype(I32), axis=1),
                              N_EXPERTS - 1)
        dest_flat = dest_rows(pad_start.astype(I32), e_t, r_t).reshape(TOP_K * t)
        slots = (t * TOP_K + jnp.arange(n_rows, dtype=I32)).at[dest_flat].set(
            jnp.arange(t * TOP_K, dtype=I32), unique_indices=True, mode="promise_in_bounds")
        yt = experts(hp.reshape(t, d // 2), slots, block_e, w_gate[l].astype(BF16), b_gate[l],
                     w_up[l].astype(BF16), b_up[l], w_down[l].astype(BF16), b_down[l])
        x = combine(w_t.T, yt, x, g2, final_norm, final=(l == n_layers - 1))
    return x
```

```python
import functools
import math

import jax
import jax.numpy as jnp
from jax import lax
from jax.experimental import pallas as pl
from jax.experimental.pallas import tpu as pltpu

F32 = jnp.float32
BF16 = jnp.bfloat16
U32 = jnp.uint32
I32 = jnp.int32

N_HEADS = 16
Q_LORA = 512
KV_LORA = 512
QK_NOPE = 128
QK_ROPE = 64
V_DIM = 128
ROPE_THETA = 10000.0
SGU_CHUNK = 128
SGU_GROUPS = 16
POOL_WINDOWS = (2, 4, 8, 16)
POOL_HALO = 16
N_EXPERTS = 32
TOP_K = 4
SWIGLU_LIMIT = 7.0
SWIGLU_ALPHA = 1.702
EPS = 1e-6
LANES = 128
ROW_TILE = 8
MOE_ROWS = 256
FLASH_TQ = 2048
FLASH_TK = 2048
FLASH_TK_DIAG = 1024
FLASH_SUB = 1024
MIB = 1024 * 1024


def _cparams(semantics, vmem_mib):
    return pltpu.CompilerParams(dimension_semantics=semantics, vmem_limit_bytes=vmem_mib * MIB)


def _adaln_kernel(c_ref, w_ref, b_ref, o_ref):
    c = c_ref[...]
    a = c * jax.nn.sigmoid(c)
    o_ref[...] = jnp.dot(a, w_ref[...], preferred_element_type=F32,
                         precision=lax.Precision.HIGHEST) + b_ref[...]


def adaln(c, w_ada, b_ada):
    n_layers, d, n = w_ada.shape
    rows = 8
    cp = jnp.zeros((rows, d), F32).at[:c.shape[0]].set(c)
    tn = 1024
    return pl.pallas_call(
        _adaln_kernel,
        out_shape=jax.ShapeDtypeStruct((n_layers, rows, n), F32),
        grid=(n_layers, n // tn),
        in_specs=[pl.BlockSpec((rows, d), lambda l, j: (0, 0)),
                  pl.BlockSpec((None, d, tn), lambda l, j: (l, 0, j)),
                  pl.BlockSpec((None, 1, tn), lambda l, j: (l, 0, j))],
        out_specs=pl.BlockSpec((None, rows, tn), lambda l, j: (l, 0, j)),
        compiler_params=_cparams(("parallel", "parallel"), 40),
        name="adaln",
    )(cp, w_ada, b_ada.reshape(n_layers, 1, n))


def _modulated_rmsnorm(x, gain, scale, shift):
    y = x * lax.rsqrt(jnp.mean(x * x, axis=-1, keepdims=True) + EPS) * gain
    return y * (1.0 + scale) + shift


def _gelu_tanh(x):
    return 0.5 * x * (1.0 + jnp.tanh(math.sqrt(2.0 / math.pi) * (x + 0.044715 * (x * x * x))))


_EPILOGUES = {None: lambda y: y, "gelu": _gelu_tanh, "sigmoid": jax.nn.sigmoid}


def _mod_norm_kernel(x_ref, g_ref, sc_ref, sh_ref, o_ref):
    o_ref[...] = _modulated_rmsnorm(x_ref[...], g_ref[...], sc_ref[...], sh_ref[...]).astype(o_ref.dtype)


def mod_norm(x, gain, scale, shift):
    b, s, d = x.shape
    tm = min(512, s)
    return pl.pallas_call(
        _mod_norm_kernel,
        out_shape=jax.ShapeDtypeStruct((b, s, d), BF16),
        grid=(b, s // tm),
        in_specs=[pl.BlockSpec((None, tm, d), lambda bi, i: (bi, i, 0)),
                  pl.BlockSpec((1, d), lambda bi, i: (0, 0)),
                  pl.BlockSpec((None, 1, d), lambda bi, i: (bi, 0, 0)),
                  pl.BlockSpec((None, 1, d), lambda bi, i: (bi, 0, 0))],
        out_specs=pl.BlockSpec((None, tm, d), lambda bi, i: (bi, i, 0)),
        compiler_params=_cparams(("parallel", "parallel"), 32),
        name="mod_norm",
    )(x, gain.reshape(1, d), scale, shift)


def _proj_kernel(h_ref, w_ref, o_ref, *, epilogue):
    y = jnp.dot(h_ref[...], w_ref[...], preferred_element_type=F32)
    o_ref[...] = _EPILOGUES[epilogue](y).astype(o_ref.dtype)


def proj(h, w, tn, epilogue=None, out_dtype=F32):
    b, s, d = h.shape
    n = w.shape[1]
    tm = min(1024, s)
    return pl.pallas_call(
        functools.partial(_proj_kernel, epilogue=epilogue),
        out_shape=jax.ShapeDtypeStruct((b, s, n), out_dtype),
        grid=(b, s // tm, n // tn),
        in_specs=[pl.BlockSpec((None, tm, d), lambda bi, i, j: (bi, i, 0)),
                  pl.BlockSpec((d, tn), lambda bi, i, j: (0, j))],
        out_specs=pl.BlockSpec((None, tm, tn), lambda bi, i, j: (bi, i, j)),
        compiler_params=_cparams(("parallel", "parallel", "parallel"), 48),
        name="proj",
    )(h, w)


def _rms(x, gain):
    return x * lax.rsqrt(jnp.mean(x * x, axis=-1, keepdims=True) + EPS) * gain


def _rope_tile(t, cc, ss):
    return t * cc + pltpu.roll(t, LANES // 2, 1) * ss


def _mla_prep_kernel(lat_ref, qn_ref, kvn_ref, cc_ref, ss_ref, wq_ref, wkv_ref, q_ref, k_ref, v_ref):
    cc = cc_ref[...]
    ss = ss_ref[...]
    cq = _rms(lat_ref[:, :Q_LORA], qn_ref[...]).astype(BF16)
    ckv = _rms(lat_ref[:, Q_LORA:Q_LORA + KV_LORA], kvn_ref[...]).astype(BF16)
    kpe = _rope_tile(lat_ref[:, Q_LORA + KV_LORA:], cc, ss).astype(BF16)
    scale = (QK_NOPE + QK_ROPE) ** -0.5 * math.log2(math.e)
    for h in range(N_HEADS):
        qres = jnp.dot(cq, wq_ref[h], preferred_element_type=F32)
        q_ref[h, :, :QK_NOPE] = (qres[:, :QK_NOPE] * scale).astype(BF16)
        q_ref[h, :, QK_NOPE:] = (_rope_tile(qres[:, QK_NOPE:], cc, ss) * scale).astype(BF16)
        kvres = jnp.dot(ckv, wkv_ref[h], preferred_element_type=F32)
        k_ref[h, :, :QK_NOPE] = kvres[:, :QK_NOPE].astype(BF16)
        k_ref[h, :, QK_NOPE:] = kpe
        v_ref[h] = kvres[:, QK_NOPE:].astype(BF16)


def mla_prep(lat, q_norm, kv_norm, cc, ss, wq, wkv):
    b, s, nlat = lat.shape
    tm = min(512, s)
    hd = QK_NOPE + LANES
    return pl.pallas_call(
        _mla_prep_kernel,
        out_shape=(jax.ShapeDtypeStruct((b, N_HEADS, s, hd), BF16),
                   jax.ShapeDtypeStruct((b, N_HEADS, s, hd), BF16),
                   jax.ShapeDtypeStruct((b, N_HEADS, s, V_DIM), BF16)),
        grid=(b, s // tm),
        in_specs=[pl.BlockSpec((None, tm, nlat), lambda bi, i: (bi, i, 0)),
                  pl.BlockSpec((1, Q_LORA), lambda bi, i: (0, 0)),
                  pl.BlockSpec((1, KV_LORA), lambda bi, i: (0, 0)),
                  pl.BlockSpec((tm, LANES), lambda bi, i: (i, 0)),
                  pl.BlockSpec((tm, LANES), lambda bi, i: (i, 0)),
                  pl.BlockSpec((N_HEADS, Q_LORA, hd), lambda bi, i: (0, 0, 0)),
                  pl.BlockSpec((N_HEADS, KV_LORA, QK_NOPE + V_DIM), lambda bi, i: (0, 0, 0))],
        out_specs=(pl.BlockSpec((None, N_HEADS, tm, hd), lambda bi, i: (bi, 0, i, 0)),
                   pl.BlockSpec((None, N_HEADS, tm, hd), lambda bi, i: (bi, 0, i, 0)),
                   pl.BlockSpec((None, N_HEADS, tm, V_DIM), lambda bi, i: (bi, 0, i, 0))),
        compiler_params=_cparams(("parallel", "parallel"), 56),
        name="mla_prep",
    )(lat, q_norm.reshape(1, -1), kv_norm.reshape(1, -1), cc, ss, wq, wkv)


def _flash_kernel(q_ref, k_ref, v_ref, o_ref, m_ref, l_ref, acc_ref, *, tk, tkd, sub):
    qi = pl.program_id(2)
    tq = q_ref.shape[0]
    sub_tiles = tq // sub
    m_ref[...] = jnp.full_like(m_ref, -jnp.inf)
    l_ref[...] = jnp.zeros_like(l_ref)
    acc_ref[...] = jnp.zeros_like(acc_ref)

    def attend(q_rows, kv_rows, diag_shift):
        s = lax.dot_general(q_ref[q_rows, :], k_ref[kv_rows, :], (((1,), (1,)), ((), ())),
                            preferred_element_type=F32)
        if diag_shift is not None:
            row = lax.broadcasted_iota(I32, s.shape, 0)
            col = lax.broadcasted_iota(I32, s.shape, 1) + diag_shift
            s = jnp.where(col <= row, s, -1e30)
        m_prev = m_ref[q_rows, :]
        m_next = jnp.maximum(m_prev, jnp.max(s, axis=-1, keepdims=True))
        alpha = jnp.exp2(m_prev - m_next)
        ps = [jnp.exp2(s[:, c * LANES:(c + 1) * LANES] - m_next) for c in range(s.shape[1] // LANES)]
        psum = ps[0]
        for pc in ps[1:]:
            psum = psum + pc
        l_ref[q_rows, :] = alpha * l_ref[q_rows, :] + psum
        p = jnp.concatenate(ps, axis=1).astype(BF16)
        acc_ref[q_rows, :] = alpha * acc_ref[q_rows, :] + jnp.dot(p, v_ref[kv_rows, :],
                                                                  preferred_element_type=F32)
        m_ref[q_rows, :] = m_next

    def body(j, carry):
        kv_rows = pl.ds(pl.multiple_of(j * tk, tk), tk)
        for t in range(sub_tiles):
            attend(pl.ds(t * sub, sub), kv_rows, None)
        return carry

    lax.fori_loop(0, qi * (tq // tk), body, 0)
    for c in range(tq // tkd):
        kv_rows = pl.ds(pl.multiple_of(qi * tq + c * tkd, tkd), tkd)
        for t in range(sub_tiles):
            first_row, last_row = t * sub, (t + 1) * sub - 1
            first_col, last_col = c * tkd, (c + 1) * tkd - 1
            if first_col > last_row:
                continue
            visible = last_col <= first_row
            attend(pl.ds(t * sub, sub), kv_rows, None if visible else first_col - first_row)
    o_ref[...] = (acc_ref[...] / jnp.sum(l_ref[...], axis=-1, keepdims=True)).astype(o_ref.dtype)


def flash(q, k, v):
    b, h, s, hd = q.shape
    tq = min(FLASH_TQ, s)
    tk = min(FLASH_TK, tq)
    sub = min(FLASH_SUB, tq)
    tkd = min(FLASH_TK_DIAG, tq)
    return pl.pallas_call(
        functools.partial(_flash_kernel, tk=tk, tkd=tkd, sub=sub),
        out_shape=jax.ShapeDtypeStruct((b, s, h * V_DIM), BF16),
        grid=(b, h, s // tq),
        in_specs=[pl.BlockSpec((None, None, tq, hd), lambda bi, hi, qi: (bi, hi, qi, 0)),
                  pl.BlockSpec((None, None, s, hd), lambda bi, hi, qi: (bi, hi, 0, 0)),
                  pl.BlockSpec((None, None, s, V_DIM), lambda bi, hi, qi: (bi, hi, 0, 0))],
        out_specs=pl.BlockSpec((None, tq, V_DIM), lambda bi, hi, qi: (bi, qi, hi)),
        scratch_shapes=[pltpu.VMEM((tq, LANES), F32), pltpu.VMEM((tq, LANES), F32),
                        pltpu.VMEM((tq, V_DIM), F32)],
        compiler_params=_cparams(("parallel", "parallel", "arbitrary"), 48),
        name="flash",
    )(q, k, v)


def _mixer_merge_kernel(z_ref, p_ref, halo_ref, gt_ref, ya_ref, sn_ref, ws_ref, bs_ref,
                        wp_ref, ps_ref, o_ref, acc_ref):
    tm, d = o_ref.shape
    gd = d // SGU_GROUPS
    tri_r = lax.broadcasted_iota(I32, (SGU_CHUNK, SGU_CHUNK), 0)
    tri_c = lax.broadcasted_iota(I32, (SGU_CHUNK, SGU_CHUNK), 1)
    tril = tri_c <= tri_r

    acc_ref[...] = gt_ref[:, :d].astype(F32) * ya_ref[...].astype(F32)

    for c in range(tm // SGU_CHUNK):
        rows = pl.ds(c * SGU_CHUNK, SGU_CHUNK)
        v = z_ref[rows, d:].astype(F32)
        mu = jnp.mean(v, axis=-1, keepdims=True)
        vc = v - mu
        var = jnp.mean(vc * vc, axis=-1, keepdims=True)
        vn = (vc * lax.rsqrt(var + EPS) * sn_ref[...]).astype(BF16)
        for g in range(SGU_GROUPS):
            cols = pl.ds(g * gd, gd)
            w = jnp.where(tril, ws_ref[g], jnp.zeros((), BF16))
            mixed = jnp.dot(w, vn[:, g * gd:(g + 1) * gd], preferred_element_type=F32) + bs_ref[:, g:g + 1]
            yb = z_ref[rows, cols].astype(F32) * mixed
            gate = gt_ref[rows, pl.ds(d + g * gd, gd)].astype(F32)
            acc_ref[rows, cols] = acc_ref[rows, cols] + gate * yb

    pgd = d // len(POOL_WINDOWS)
    first = pl.program_id(1) == 0
    pos = (pl.program_id(1) * tm + lax.broadcasted_iota(I32, (tm, 1), 0) + 1).astype(F32)
    for gi, win in enumerate(POOL_WINDOWS):
        cols = pl.ds(gi * pgd, pgd)
        xg = p_ref[:, cols]
        halo = jnp.where(first, 0.0, halo_ref[:, cols])
        e = jnp.concatenate([halo, xg], axis=0)
        span = 1
        while span < win:
            e = e + pltpu.roll(e, span, 0)
            span *= 2
        wsum = e[POOL_HALO:]
        dlt = wsum / jnp.minimum(pos, float(win)) - xg
        yc = jnp.dot(dlt.astype(BF16), wp_ref[gi], preferred_element_type=F32) * ps_ref[:, cols]
        gate = gt_ref[:, pl.ds(2 * d + gi * pgd, pgd)].astype(F32)
        o_ref[:, cols] = (acc_ref[:, cols] + gate * yc).astype(o_ref.dtype)


def mixer_merge(z, p, gates, ya, sgu_norm, ws, bs_t, wp, pool_scale):
    b, s, d = p.shape
    tm = min(256, s)
    hb = tm // POOL_HALO
    return pl.pallas_call(
        _mixer_merge_kernel,
        out_shape=jax.ShapeDtypeStruct((b, s, d), BF16),
        grid=(b, s // tm),
        in_specs=[pl.BlockSpec((None, tm, 2 * d), lambda bi, i: (bi, i, 0)),
                  pl.BlockSpec((None, tm, d), lambda bi, i: (bi, i, 0)),
                  pl.BlockSpec((None, POOL_HALO, d), lambda bi, i: (bi, jnp.maximum(i * hb - 1, 0), 0)),
                  pl.BlockSpec((None, tm, 3 * d), lambda bi, i: (bi, i, 0)),
                  pl.BlockSpec((None, tm, d), lambda bi, i: (bi, i, 0)),
                  pl.BlockSpec((1, d), lambda bi, i: (0, 0)),
                  pl.BlockSpec(ws.shape, lambda bi, i: (0, 0, 0)),
                  pl.BlockSpec(bs_t.shape, lambda bi, i: (0, 0)),
                  pl.BlockSpec(wp.shape, lambda bi, i: (0, 0, 0)),
                  pl.BlockSpec((1, d), lambda bi, i: (0, 0))],
        out_specs=pl.BlockSpec((None, tm, d), lambda bi, i: (bi, i, 0)),
        scratch_shapes=[pltpu.VMEM((tm, d), F32)],
        compiler_params=_cparams(("parallel", "arbitrary"), 56),
        name="mixer_merge",
    )(z, p, p, gates, ya, sgu_norm.reshape(1, d), ws, bs_t, wp, pool_scale.reshape(1, d))


def _matmul_res_kernel(a_ref, w_ref, x_ref, g_ref, o_ref):
    y = jnp.dot(a_ref[...], w_ref[...], preferred_element_type=F32)
    o_ref[...] = x_ref[...] + g_ref[...] * y


def matmul_res(a, w, x, gate):
    b, s, d = x.shape
    tm = min(512, s)
    return pl.pallas_call(
        _matmul_res_kernel,
        out_shape=jax.ShapeDtypeStruct((b, s, d), F32),
        grid=(b, s // tm),
        in_specs=[pl.BlockSpec((None, tm, d), lambda bi, i: (bi, i, 0)),
                  pl.BlockSpec((d, d), lambda bi, i: (0, 0)),
                  pl.BlockSpec((None, tm, d), lambda bi, i: (bi, i, 0)),
                  pl.BlockSpec((None, 1, d), lambda bi, i: (bi, 0, 0))],
        out_specs=pl.BlockSpec((None, tm, d), lambda bi, i: (bi, i, 0)),
        compiler_params=_cparams(("parallel", "parallel"), 56),
        name="matmul_res",
    )(a, w, x, gate)


def _pack_bf16_pair(a, b):
    pa = lax.bitcast_convert_type(a.astype(BF16).astype(F32), U32)
    pb = lax.bitcast_convert_type(b.astype(BF16).astype(F32), U32)
    return pa | (pb >> 16)


def _unpack_bf16_pair(w):
    hi = lax.bitcast_convert_type(w & jnp.uint32(0xFFFF0000), F32)
    lo = lax.bitcast_convert_type(w << 16, F32)
    return hi, lo


def _store_row_tiles(ref, rows):
    n, w = rows.shape
    for s in range(w // LANES):
        ref[pl.ds(s, n, stride=ROW_TILE), :] = rows[:, s * LANES:(s + 1) * LANES]


def _load_row_tiles(ref):
    n = ref.shape[0] // ROW_TILE
    return jnp.concatenate([ref[pl.ds(s, n, stride=ROW_TILE), :] for s in range(ROW_TILE)], axis=1)


def _router_kernel(x_ref, g_ref, sc_ref, sh_ref, wr_ref, br_ref,
                   hp_ref, e_ref, w_ref, r_ref, cnt_ref, carry_ref):
    tm, d = x_ref.shape
    half = d // 2

    @pl.when((pl.program_id(0) == 0) & (pl.program_id(1) == 0))
    def _():
        carry_ref[...] = jnp.zeros_like(carry_ref)

    h = _modulated_rmsnorm(x_ref[...], g_ref[...], sc_ref[...], sh_ref[...])
    _store_row_tiles(hp_ref, _pack_bf16_pair(h[:, :half], h[:, half:]))

    logits = lax.dot_general(wr_ref[...], h, (((1,), (1,)), ((), ())),
                             preferred_element_type=F32, precision=lax.Precision.HIGHEST) + br_ref[...]
    eio = lax.broadcasted_iota(I32, logits.shape, 0).astype(F32)
    onehot = jnp.zeros(logits.shape, F32)
    vals, sels = [], []
    rem = logits
    for k in range(TOP_K):
        m = jnp.max(rem, axis=0, keepdims=True)
        idx = jnp.min(jnp.where(rem == m, eio, float(N_EXPERTS)), axis=0, keepdims=True)
        sel = eio == idx
        rem = jnp.where(sel, -jnp.inf, rem)
        onehot = onehot + jnp.where(sel, 1.0, 0.0)
        vals.append(m)
        sels.append(sel)
        e_ref[k:k + 1, :] = idx.astype(I32)
    ex = [jnp.exp(v - vals[0]) for v in vals]
    den = ex[0] + ex[1] + ex[2] + ex[3]
    for k in range(TOP_K):
        w_ref[k:k + 1, :] = ex[k] / den

    before = lax.broadcasted_iota(I32, (tm, tm), 0) < lax.broadcasted_iota(I32, (tm, tm), 1)
    cum = jnp.dot(onehot.astype(BF16), jnp.where(before, 1.0, 0.0).astype(BF16),
                  preferred_element_type=F32) + carry_ref[...]
    for k in range(TOP_K):
        r_ref[k:k + 1, :] = jnp.sum(jnp.where(sels[k], cum, 0.0), axis=0, keepdims=True).astype(I32)
    carry_ref[...] = carry_ref[...] + jnp.sum(onehot, axis=1, keepdims=True)
    cnt_ref[...] = jnp.broadcast_to(carry_ref[...], cnt_ref.shape).astype(I32)


def router(x, gain, scale, shift, wr_t, b_router):
    b, s, d = x.shape
    tm = min(512, s)
    nt = s // tm
    t = b * s
    tok = lambda bi, i: (0, bi * nt + i)
    return pl.pallas_call(
        _router_kernel,
        out_shape=(jax.ShapeDtypeStruct((b, s * ROW_TILE, LANES), U32),
                   jax.ShapeDtypeStruct((TOP_K, t), I32),
                   jax.ShapeDtypeStruct((TOP_K, t), F32),
                   jax.ShapeDtypeStruct((TOP_K, t), I32),
                   jax.ShapeDtypeStruct((N_EXPERTS, LANES), I32)),
        grid=(b, nt),
        in_specs=[pl.BlockSpec((None, tm, d), lambda bi, i: (bi, i, 0)),
                  pl.BlockSpec((1, d), lambda bi, i: (0, 0)),
                  pl.BlockSpec((None, 1, d), lambda bi, i: (bi, 0, 0)),
                  pl.BlockSpec((None, 1, d), lambda bi, i: (bi, 0, 0)),
                  pl.BlockSpec((N_EXPERTS, d), lambda bi, i: (0, 0)),
                  pl.BlockSpec((N_EXPERTS, 1), lambda bi, i: (0, 0))],
        out_specs=(pl.BlockSpec((None, tm * ROW_TILE, LANES), lambda bi, i: (bi, i, 0)),
                   pl.BlockSpec((TOP_K, tm), tok),
                   pl.BlockSpec((TOP_K, tm), tok),
                   pl.BlockSpec((TOP_K, tm), tok),
                   pl.BlockSpec((N_EXPERTS, LANES), lambda bi, i: (0, 0))),
        scratch_shapes=[pltpu.VMEM((N_EXPERTS, 1), F32)],
        compiler_params=_cparams(("arbitrary", "arbitrary"), 40),
        name="router",
    )(x, gain.reshape(1, d), scale, shift, wr_t, b_router.reshape(N_EXPERTS, 1))


def _dest_kernel(ps_ref, e_ref, r_ref, o_ref):
    e = e_ref[...]
    base = jnp.zeros(e.shape, I32)
    for j in range(N_EXPERTS):
        base = jnp.where(e == j, ps_ref[j], base)
    o_ref[...] = base + r_ref[...]


def dest_rows(pad_start, e_t, r_t):
    return pl.pallas_call(
        _dest_kernel,
        out_shape=jax.ShapeDtypeStruct(e_t.shape, I32),
        grid_spec=pltpu.PrefetchScalarGridSpec(
            num_scalar_prefetch=1, grid=(1,),
            in_specs=[pl.BlockSpec(e_t.shape, lambda i, ps: (0, 0)),
                      pl.BlockSpec(e_t.shape, lambda i, ps: (0, 0))],
            out_specs=pl.BlockSpec(e_t.shape, lambda i, ps: (0, 0))),
        compiler_params=_cparams(("arbitrary",), 16),
        name="dest",
    )(pad_start, e_t, r_t)


def _experts_kernel(be_ref, src_first_ref, src_next_ref, slot_prev_ref, slot_last_ref, hp_ref,
                    wg_ref, bg_ref, wu_ref, bu_ref, wd_ref, bd_ref,
                    yt_ref, xbuf, ybuf, gsem, ssem, *, spare_base):
    del be_ref
    i = pl.program_id(0)
    last = pl.num_programs(0) - 1
    cur = lax.rem(i, 2)
    nxt = 1 - cur
    rows = ybuf.shape[1] // ROW_TILE

    def tile(ref, offset):
        return ref.at[pl.ds(pl.multiple_of(offset, ROW_TILE), ROW_TILE)]

    def gather_copy(src, r, buf):
        return pltpu.make_async_copy(tile(hp_ref, src), tile(xbuf.at[buf], r * ROW_TILE), gsem.at[buf])

    def scatter_copy(dst, r, buf):
        return pltpu.make_async_copy(tile(ybuf.at[buf], r * ROW_TILE), tile(yt_ref, dst), ssem.at[buf])

    def wait_gathers(buf):
        for r in range(rows):
            gather_copy(0, r, buf).wait()

    def wait_scatters(buf):
        for r in range(rows):
            scatter_copy(0, r, buf).wait()

    @pl.when(i == 0)
    def _():
        for r in range(rows):
            gather_copy(src_first_ref[r], r, 0).start()
        ybuf[1] = jnp.zeros(ybuf.shape[1:], ybuf.dtype)

    wait_gathers(cur)

    @pl.when(i >= 1)
    def _():
        wait_scatters(cur)

    hi, lo = _unpack_bf16_pair(_load_row_tiles(xbuf.at[cur]))
    x = jnp.concatenate([hi.astype(BF16), lo.astype(BF16)], axis=1)
    for r in range(rows):
        gather_copy(src_next_ref[r], r, nxt).start()
        dst = jnp.where(i >= 1, slot_prev_ref[r], spare_base + r * ROW_TILE)
        scatter_copy(dst, r, nxt).start()

    gl = jnp.dot(x, wg_ref[...], preferred_element_type=F32) + bg_ref[...]
    li = jnp.dot(x, wu_ref[...], preferred_element_type=F32) + bu_ref[...]
    gl = jnp.minimum(gl, SWIGLU_LIMIT)
    li = jnp.clip(li, -SWIGLU_LIMIT, SWIGLU_LIMIT)
    act = gl * jax.nn.sigmoid(SWIGLU_ALPHA * gl) * (li + 1.0)
    y = jnp.dot(act.astype(BF16), wd_ref[...], preferred_element_type=F32) + bd_ref[...]
    half = y.shape[1] // 2
    _store_row_tiles(ybuf.at[cur], _pack_bf16_pair(y[:, :half], y[:, half:]))

    @pl.when(i == last)
    def _():
        for r in range(rows):
            scatter_copy(slot_last_ref[r], r, cur).start()
        wait_scatters(nxt)
        wait_scatters(cur)
        wait_gathers(nxt)


def experts(hp, slots, block_e, wg, bg, wu, bu, wd, bd):
    t = hp.shape[0] // ROW_TILE
    n_rows = slots.shape[0]
    n_blocks = n_rows // MOE_ROWS
    n_e, d, f = wg.shape
    spare_base = t * TOP_K + n_rows
    src = (slots % t) * ROW_TILE
    slots = slots * ROW_TILE
    kern = functools.partial(_experts_kernel, spare_base=spare_base * ROW_TILE)
    smem_block = lambda idx: pl.BlockSpec((MOE_ROWS,), idx, memory_space=pltpu.SMEM)
    return pl.pallas_call(
        kern,
        out_shape=jax.ShapeDtypeStruct(((spare_base + MOE_ROWS) * ROW_TILE, LANES), U32),
        grid_spec=pltpu.PrefetchScalarGridSpec(
            num_scalar_prefetch=1, grid=(n_blocks,),
            in_specs=[smem_block(lambda i, be: (0,)),
                      smem_block(lambda i, be: (jnp.minimum(i + 1, n_blocks - 1),)),
                      smem_block(lambda i, be: (jnp.maximum(i - 1, 0),)),
                      smem_block(lambda i, be: (n_blocks - 1,)),
                      pl.BlockSpec(memory_space=pl.ANY),
                      pl.BlockSpec((None, d, f), lambda i, be: (be[i], 0, 0)),
                      pl.BlockSpec((None, 1, f), lambda i, be: (be[i], 0, 0)),
                      pl.BlockSpec((None, d, f), lambda i, be: (be[i], 0, 0)),
                      pl.BlockSpec((None, 1, f), lambda i, be: (be[i], 0, 0)),
                      pl.BlockSpec((None, f, d), lambda i, be: (be[i], 0, 0)),
                      pl.BlockSpec((None, 1, d), lambda i, be: (be[i], 0, 0))],
            out_specs=pl.BlockSpec(memory_space=pl.ANY),
            scratch_shapes=[pltpu.VMEM((2, MOE_ROWS * ROW_TILE, LANES), U32)] * 2 + [
                pltpu.SemaphoreType.DMA((2,)), pltpu.SemaphoreType.DMA((2,))]),
        compiler_params=_cparams(("arbitrary",), 56),
        name="experts",
    )(block_e, src, src, slots, slots, hp, wg, bg.reshape(n_e, 1, f), wu, bu.reshape(n_e, 1, f), wd,
      bd.reshape(n_e, 1, d))


def _combine_kernel(w_ref, y0_ref, y1_ref, y2_ref, y3_ref, x_ref, g_ref, fn_ref, o_ref, *, final):
    half = x_ref.shape[1] // 2
    acc_hi = None
    for k, y_ref in enumerate((y0_ref, y1_ref, y2_ref, y3_ref)):
        hi, lo = _unpack_bf16_pair(_load_row_tiles(y_ref))
        wk = w_ref[:, k:k + 1]
        if acc_hi is None:
            acc_hi, acc_lo = wk * hi, wk * lo
        else:
            acc_hi, acc_lo = acc_hi + wk * hi, acc_lo + wk * lo
    out = x_ref[...] + g_ref[...] * jnp.concatenate([acc_hi, acc_lo], axis=1)
    if final:
        out = _rms(out, fn_ref[...])
    o_ref[...] = out


def combine(w_tok, yt, x, gate, final_norm, final):
    b, s, d = x.shape
    tm = min(512, s)
    nt = s // tm
    tiles = b * nt
    y_spec = lambda k: pl.BlockSpec((tm * ROW_TILE, LANES), lambda bi, i: (k * tiles + bi * nt + i, 0))
    return pl.pallas_call(
        functools.partial(_combine_kernel, final=final),
        out_shape=jax.ShapeDtypeStruct((b, s, d), F32),
        grid=(b, nt),
        in_specs=[pl.BlockSpec((tm, TOP_K), lambda bi, i: (bi * nt + i, 0)),
                  y_spec(0), y_spec(1), y_spec(2), y_spec(3),
                  pl.BlockSpec((None, tm, d), lambda bi, i: (bi, i, 0)),
                  pl.BlockSpec((None, 1, d), lambda bi, i: (bi, 0, 0)),
                  pl.BlockSpec((1, d), lambda bi, i: (0, 0))],
        out_specs=pl.BlockSpec((None, tm, d), lambda bi, i: (bi, i, 0)),
        compiler_params=_cparams(("parallel", "parallel"), 48),
        name="combine",
    )(w_tok, yt, yt, yt, yt, x, gate, final_norm.reshape(1, d))


def _swap_halves(w):
    half = w.shape[-1] // 2
    return jnp.concatenate([w[..., half:], w[..., :half]], axis=-1)


def _rope_tables(s):
    pos = jnp.arange(s, dtype=F32)
    inv_freq = ROPE_THETA ** (-jnp.arange(0, QK_ROPE, 2, dtype=F32) / QK_ROPE)
    ang = pos[:, None] * inv_freq[None, :]
    cos, sin = jnp.cos(ang), jnp.sin(ang)
    zero = jnp.zeros((s, LANES - QK_ROPE), F32)
    return (jnp.concatenate([cos, cos, zero], axis=1), jnp.concatenate([-sin, sin, zero], axis=1))


def kernel(x, c, attn_norm, ffn_norm, w_ada, b_ada, w_in, q_norm, kv_norm, w_uq, w_ukv, sgu_norm, w_s, b_s,
           w_pool, pool_scale, w_o, w_router, b_router, w_gate, b_gate, w_up, b_up, w_down, b_down, final_norm):
    b, s, d = x.shape
    n_layers = w_ada.shape[0]
    t = b * s
    mla_in = Q_LORA + KV_LORA + QK_ROPE
    mods = adaln(c, w_ada, b_ada)
    cc, ss = _rope_tables(s)
    n_blocks = t * TOP_K // MOE_ROWS + N_EXPERTS
    n_rows = n_blocks * MOE_ROWS

    for l in range(n_layers):
        sh1, sc1, g1, sh2, sc2, g2 = [mods[l, :b, i * d:(i + 1) * d].reshape(b, 1, d) for i in range(6)]

        wl = w_in[l]
        k_rope_cols = wl[:, Q_LORA + KV_LORA:mla_in]
        w_lat = jnp.concatenate([wl[:, :mla_in], _swap_halves(k_rope_cols)], axis=1).astype(BF16)
        w_z = wl[:, mla_in:mla_in + 2 * d].astype(BF16)
        w_p = wl[:, mla_in + 2 * d:mla_in + 3 * d].astype(BF16)
        w_g = wl[:, mla_in + 3 * d:].astype(BF16)
        h = mod_norm(x, attn_norm[l], sc1, sh1)
        lat = proj(h, w_lat, w_lat.shape[1])
        z = proj(h, w_z, 1024, "gelu", BF16)
        p = proj(h, w_p, 1024)
        gates = proj(h, w_g, 1024, "sigmoid", BF16)

        wq = w_uq[l]
        wq = jnp.concatenate([wq, _swap_halves(wq[..., QK_NOPE:])], axis=-1)
        wq = wq.transpose(1, 0, 2).astype(BF16)
        wkv = w_ukv[l].transpose(1, 0, 2).astype(BF16)
        q, k, v = mla_prep(lat, q_norm[l], kv_norm[l], cc, ss, wq, wkv)
        ya = flash(q, k, v)

        merged = mixer_merge(z, p, gates, ya, sgu_norm[l], w_s[l].astype(BF16), b_s[l].T,
                             w_pool[l].astype(BF16), pool_scale[l])
        x = matmul_res(merged, w_o[l].astype(BF16), x, g1)

        hp, e_t, w_t, r_t, cnt = router(x, ffn_norm[l], sc2, sh2, w_router[l].T, b_router[l])
        counts = cnt[:, 0]
        padded = ((counts + MOE_ROWS - 1) // MOE_ROWS) * MOE_ROWS
        pad_end = jnp.cumsum(padded)
        pad_start = pad_end - padded
        block_start = jnp.arange(n_blocks, dtype=I32) * MOE_ROWS
        block_e = jnp.minimum(jnp.sum((pad_end[None, :] <= block_start[:, None]).astype(I32), axis=1),
                              N_EXPERTS - 1)
        dest_flat = dest_rows(pad_start.astype(I32), e_t, r_t).reshape(TOP_K * t)
        slots = (t * TOP_K + jnp.arange(n_rows, dtype=I32)).at[dest_flat].set(
            jnp.arange(t * TOP_K, dtype=I32), unique_indices=True, mode="promise_in_bounds")
        yt = experts(hp.reshape(t * ROW_TILE, LANES), slots, block_e, w_gate[l].astype(BF16), b_gate[l],
                     w_up[l].astype(BF16), b_up[l], w_down[l].astype(BF16), b_down[l])
        x = combine(w_t.T, yt, x, g2, final_norm, final=(l == n_layers - 1))
    return x
```

```python
import functools
import math

import jax
import jax.numpy as jnp
from jax import lax
from jax.experimental import pallas as pl
from jax.experimental.pallas import tpu as pltpu

F32 = jnp.float32
BF16 = jnp.bfloat16
U32 = jnp.uint32
I32 = jnp.int32

N_HEADS = 16
Q_LORA = 512
KV_LORA = 512
QK_NOPE = 128
QK_ROPE = 64
V_DIM = 128
ROPE_THETA = 10000.0
SGU_CHUNK = 128
SGU_GROUPS = 16
POOL_WINDOWS = (2, 4, 8, 16)
POOL_HALO = 16
N_EXPERTS = 32
TOP_K = 4
SWIGLU_LIMIT = 7.0
SWIGLU_ALPHA = 1.702
EPS = 1e-6
LANES = 128
ROW_TILE = 8
MOE_ROWS = 256
FLASH_TQ = 2048
FLASH_TK = 2048
FLASH_TK_DIAG = 1024
FLASH_SUB = 1024
MIB = 1024 * 1024


def _cparams(semantics, vmem_mib):
    return pltpu.CompilerParams(dimension_semantics=semantics, vmem_limit_bytes=vmem_mib * MIB)


def _adaln_kernel(c_ref, w_ref, b_ref, o_ref):
    c = c_ref[...]
    a = c * jax.nn.sigmoid(c)
    o_ref[...] = jnp.dot(a, w_ref[...], preferred_element_type=F32,
                         precision=lax.Precision.HIGHEST) + b_ref[...]


def adaln(c, w_ada, b_ada):
    n_layers, d, n = w_ada.shape
    rows = 8
    cp = jnp.zeros((rows, d), F32).at[:c.shape[0]].set(c)
    tn = 1024
    return pl.pallas_call(
        _adaln_kernel,
        out_shape=jax.ShapeDtypeStruct((n_layers, rows, n), F32),
        grid=(n_layers, n // tn),
        in_specs=[pl.BlockSpec((rows, d), lambda l, j: (0, 0)),
                  pl.BlockSpec((None, d, tn), lambda l, j: (l, 0, j)),
                  pl.BlockSpec((None, 1, tn), lambda l, j: (l, 0, j))],
        out_specs=pl.BlockSpec((None, rows, tn), lambda l, j: (l, 0, j)),
        compiler_params=_cparams(("parallel", "parallel"), 40),
        name="adaln",
    )(cp, w_ada, b_ada.reshape(n_layers, 1, n))


def _modulated_rmsnorm(x, gain, scale, shift):
    y = x * lax.rsqrt(jnp.mean(x * x, axis=-1, keepdims=True) + EPS) * gain
    return y * (1.0 + scale) + shift


def _gelu_tanh(x):
    return 0.5 * x * (1.0 + jnp.tanh(math.sqrt(2.0 / math.pi) * (x + 0.044715 * (x * x * x))))


_EPILOGUES = {None: lambda y: y, "gelu": _gelu_tanh, "sigmoid": jax.nn.sigmoid}


def _mod_norm_kernel(x_ref, g_ref, sc_ref, sh_ref, o_ref):
    o_ref[...] = _modulated_rmsnorm(x_ref[...], g_ref[...], sc_ref[...], sh_ref[...]).astype(o_ref.dtype)


def mod_norm(x, gain, scale, shift):
    b, s, d = x.shape
    tm = min(512, s)
    return pl.pallas_call(
        _mod_norm_kernel,
        out_shape=jax.ShapeDtypeStruct((b, s, d), BF16),
        grid=(b, s // tm),
        in_specs=[pl.BlockSpec((None, tm, d), lambda bi, i: (bi, i, 0)),
                  pl.BlockSpec((1, d), lambda bi, i: (0, 0)),
                  pl.BlockSpec((None, 1, d), lambda bi, i: (bi, 0, 0)),
                  pl.BlockSpec((None, 1, d), lambda bi, i: (bi, 0, 0))],
        out_specs=pl.BlockSpec((None, tm, d), lambda bi, i: (bi, i, 0)),
        compiler_params=_cparams(("parallel", "parallel"), 32),
        name="mod_norm",
    )(x, gain.reshape(1, d), scale, shift)


def _proj_kernel(h_ref, w_ref, o_ref, *, epilogue):
    y = jnp.dot(h_ref[...], w_ref[...], preferred_element_type=F32)
    o_ref[...] = _EPILOGUES[epilogue](y).astype(o_ref.dtype)


def proj(h, w, tn, epilogue=None, out_dtype=F32):
    b, s, d = h.shape
    n = w.shape[1]
    tm = min(1024, s)
    return pl.pallas_call(
        functools.partial(_proj_kernel, epilogue=epilogue),
        out_shape=jax.ShapeDtypeStruct((b, s, n), out_dtype),
        grid=(b, s // tm, n // tn),
        in_specs=[pl.BlockSpec((None, tm, d), lambda bi, i, j: (bi, i, 0)),
                  pl.BlockSpec((d, tn), lambda bi, i, j: (0, j))],
        out_specs=pl.BlockSpec((None, tm, tn), lambda bi, i, j: (bi, i, j)),
        compiler_params=_cparams(("parallel", "parallel", "parallel"), 48),
        name="proj",
    )(h, w)


def _rms(x, gain):
    return x * lax.rsqrt(jnp.mean(x * x, axis=-1, keepdims=True) + EPS) * gain


def _rope_tile(t, cc, ss):
    return t * cc + pltpu.roll(t, LANES // 2, 1) * ss


def _mla_prep_kernel(lat_ref, qn_ref, kvn_ref, cc_ref, ss_ref, wq_ref, wkv_ref, q_ref, k_ref, v_ref):
    cc = cc_ref[...]
    ss = ss_ref[...]
    cq = _rms(lat_ref[:, :Q_LORA], qn_ref[...]).astype(BF16)
    ckv = _rms(lat_ref[:, Q_LORA:Q_LORA + KV_LORA], kvn_ref[...]).astype(BF16)
    kpe = _rope_tile(lat_ref[:, Q_LORA + KV_LORA:], cc, ss).astype(BF16)
    scale = (QK_NOPE + QK_ROPE) ** -0.5 * math.log2(math.e)
    for h in range(N_HEADS):
        qres = jnp.dot(cq, wq_ref[h], preferred_element_type=F32)
        q_ref[h, :, :QK_NOPE] = (qres[:, :QK_NOPE] * scale).astype(BF16)
        q_ref[h, :, QK_NOPE:] = (_rope_tile(qres[:, QK_NOPE:], cc, ss) * scale).astype(BF16)
        kvres = jnp.dot(ckv, wkv_ref[h], preferred_element_type=F32)
        k_ref[h, :, :QK_NOPE] = kvres[:, :QK_NOPE].astype(BF16)
        k_ref[h, :, QK_NOPE:] = kpe
        v_ref[h] = kvres[:, QK_NOPE:].astype(BF16)


def mla_prep(lat, q_norm, kv_norm, cc, ss, wq, wkv):
    b, s, nlat = lat.shape
    tm = min(512, s)
    hd = QK_NOPE + LANES
    return pl.pallas_call(
        _mla_prep_kernel,
        out_shape=(jax.ShapeDtypeStruct((b, N_HEADS, s, hd), BF16),
                   jax.ShapeDtypeStruct((b, N_HEADS, s, hd), BF16),
                   jax.ShapeDtypeStruct((b, N_HEADS, s, V_DIM), BF16)),
        grid=(b, s // tm),
        in_specs=[pl.BlockSpec((None, tm, nlat), lambda bi, i: (bi, i, 0)),
                  pl.BlockSpec((1, Q_LORA), lambda bi, i: (0, 0)),
                  pl.BlockSpec((1, KV_LORA), lambda bi, i: (0, 0)),
                  pl.BlockSpec((tm, LANES), lambda bi, i: (i, 0)),
                  pl.BlockSpec((tm, LANES), lambda bi, i: (i, 0)),
                  pl.BlockSpec((N_HEADS, Q_LORA, hd), lambda bi, i: (0, 0, 0)),
                  pl.BlockSpec((N_HEADS, KV_LORA, QK_NOPE + V_DIM), lambda bi, i: (0, 0, 0))],
        out_specs=(pl.BlockSpec((None, N_HEADS, tm, hd), lambda bi, i: (bi, 0, i, 0)),
                   pl.BlockSpec((None, N_HEADS, tm, hd), lambda bi, i: (bi, 0, i, 0)),
                   pl.BlockSpec((None, N_HEADS, tm, V_DIM), lambda bi, i: (bi, 0, i, 0))),
        compiler_params=_cparams(("parallel", "parallel"), 56),
        name="mla_prep",
    )(lat, q_norm.reshape(1, -1), kv_norm.reshape(1, -1), cc, ss, wq, wkv)


def _flash_kernel(q_ref, k_ref, v_ref, o_ref, m_ref, l_ref, acc_ref, *, tk, tkd, sub):
    qi = pl.program_id(2)
    tq = q_ref.shape[0]
    sub_tiles = tq // sub
    m_ref[...] = jnp.full_like(m_ref, -jnp.inf)
    l_ref[...] = jnp.zeros_like(l_ref)
    acc_ref[...] = jnp.zeros_like(acc_ref)

    def attend(q_rows, kv_rows, diag_shift):
        s = lax.dot_general(q_ref[q_rows, :], k_ref[kv_rows, :], (((1,), (1,)), ((), ())),
                            preferred_element_type=F32)
        if diag_shift is not None:
            row = lax.broadcasted_iota(I32, s.shape, 0)
            col = lax.broadcasted_iota(I32, s.shape, 1) + diag_shift
            s = jnp.where(col <= row, s, -1e30)
        m_prev = m_ref[q_rows, :]
        m_next = jnp.maximum(m_prev, jnp.max(s, axis=-1, keepdims=True))
        alpha = jnp.exp2(m_prev - m_next)
        ps = [jnp.exp2(s[:, c * LANES:(c + 1) * LANES] - m_next) for c in range(s.shape[1] // LANES)]
        psum = ps[0]
        for pc in ps[1:]:
            psum = psum + pc
        l_ref[q_rows, :] = alpha * l_ref[q_rows, :] + psum
        p = jnp.concatenate(ps, axis=1).astype(BF16)
        acc_ref[q_rows, :] = alpha * acc_ref[q_rows, :] + jnp.dot(p, v_ref[kv_rows, :],
                                                                  preferred_element_type=F32)
        m_ref[q_rows, :] = m_next

    def body(j, carry):
        kv_rows = pl.ds(pl.multiple_of(j * tk, tk), tk)
        for t in range(sub_tiles):
            attend(pl.ds(t * sub, sub), kv_rows, None)
        return carry

    lax.fori_loop(0, qi * (tq // tk), body, 0)
    for c in range(tq // tkd):
        kv_rows = pl.ds(pl.multiple_of(qi * tq + c * tkd, tkd), tkd)
        for t in range(sub_tiles):
            first_row, last_row = t * sub, (t + 1) * sub - 1
            first_col, last_col = c * tkd, (c + 1) * tkd - 1
            if first_col > last_row:
                continue
            visible = last_col <= first_row
            attend(pl.ds(t * sub, sub), kv_rows, None if visible else first_col - first_row)
    o_ref[...] = (acc_ref[...] / jnp.sum(l_ref[...], axis=-1, keepdims=True)).astype(o_ref.dtype)


def flash(q, k, v):
    b, h, s, hd = q.shape
    tq = min(FLASH_TQ, s)
    tk = min(FLASH_TK, tq)
    sub = min(FLASH_SUB, tq)
    tkd = min(FLASH_TK_DIAG, tq)
    return pl.pallas_call(
        functools.partial(_flash_kernel, tk=tk, tkd=tkd, sub=sub),
        out_shape=jax.ShapeDtypeStruct((b, s, h * V_DIM), BF16),
        grid=(b, h, s // tq),
        in_specs=[pl.BlockSpec((None, None, tq, hd), lambda bi, hi, qi: (bi, hi, qi, 0)),
                  pl.BlockSpec((None, None, s, hd), lambda bi, hi, qi: (bi, hi, 0, 0)),
                  pl.BlockSpec((None, None, s, V_DIM), lambda bi, hi, qi: (bi, hi, 0, 0))],
        out_specs=pl.BlockSpec((None, tq, V_DIM), lambda bi, hi, qi: (bi, qi, hi)),
        scratch_shapes=[pltpu.VMEM((tq, LANES), F32), pltpu.VMEM((tq, LANES), F32),
                        pltpu.VMEM((tq, V_DIM), F32)],
        compiler_params=_cparams(("parallel", "parallel", "arbitrary"), 48),
        name="flash",
    )(q, k, v)


def _mixer_merge_kernel(z_ref, p_ref, halo_ref, gt_ref, ya_ref, sn_ref, ws_ref, bs_ref,
                        wp_ref, ps_ref, o_ref, acc_ref):
    tm, d = o_ref.shape
    gd = d // SGU_GROUPS
    tri_r = lax.broadcasted_iota(I32, (SGU_CHUNK, SGU_CHUNK), 0)
    tri_c = lax.broadcasted_iota(I32, (SGU_CHUNK, SGU_CHUNK), 1)
    tril = tri_c <= tri_r

    acc_ref[...] = gt_ref[:, :d].astype(F32) * ya_ref[...].astype(F32)

    for c in range(tm // SGU_CHUNK):
        rows = pl.ds(c * SGU_CHUNK, SGU_CHUNK)
        v = z_ref[rows, d:].astype(F32)
        mu = jnp.mean(v, axis=-1, keepdims=True)
        vc = v - mu
        var = jnp.mean(vc * vc, axis=-1, keepdims=True)
        vn = (vc * lax.rsqrt(var + EPS) * sn_ref[...]).astype(BF16)
        for g in range(SGU_GROUPS):
            cols = pl.ds(g * gd, gd)
            w = jnp.where(tril, ws_ref[g], jnp.zeros((), BF16))
            mixed = jnp.dot(w, vn[:, g * gd:(g + 1) * gd], preferred_element_type=F32) + bs_ref[:, g:g + 1]
            yb = z_ref[rows, cols].astype(F32) * mixed
            gate = gt_ref[rows, pl.ds(d + g * gd, gd)].astype(F32)
            acc_ref[rows, cols] = acc_ref[rows, cols] + gate * yb

    pgd = d // len(POOL_WINDOWS)
    first = pl.program_id(1) == 0
    pos = (pl.program_id(1) * tm + lax.broadcasted_iota(I32, (tm, 1), 0) + 1).astype(F32)
    for gi, win in enumerate(POOL_WINDOWS):
        cols = pl.ds(gi * pgd, pgd)
        xg = p_ref[:, cols]
        halo = jnp.where(first, 0.0, halo_ref[:, cols])
        e = jnp.concatenate([halo, xg], axis=0)
        span = 1
        while span < win:
            e = e + pltpu.roll(e, span, 0)
            span *= 2
        wsum = e[POOL_HALO:]
        dlt = wsum / jnp.minimum(pos, float(win)) - xg
        yc = jnp.dot(dlt.astype(BF16), wp_ref[gi], preferred_element_type=F32) * ps_ref[:, cols]
        gate = gt_ref[:, pl.ds(2 * d + gi * pgd, pgd)].astype(F32)
        o_ref[:, cols] = (acc_ref[:, cols] + gate * yc).astype(o_ref.dtype)


def mixer_merge(z, p, gates, ya, sgu_norm, ws, bs_t, wp, pool_scale):
    b, s, d = p.shape
    tm = min(256, s)
    hb = tm // POOL_HALO
    return pl.pallas_call(
        _mixer_merge_kernel,
        out_shape=jax.ShapeDtypeStruct((b, s, d), BF16),
        grid=(b, s // tm),
        in_specs=[pl.BlockSpec((None, tm, 2 * d), lambda bi, i: (bi, i, 0)),
                  pl.BlockSpec((None, tm, d), lambda bi, i: (bi, i, 0)),
                  pl.BlockSpec((None, POOL_HALO, d), lambda bi, i: (bi, jnp.maximum(i * hb - 1, 0), 0)),
                  pl.BlockSpec((None, tm, 3 * d), lambda bi, i: (bi, i, 0)),
                  pl.BlockSpec((None, tm, d), lambda bi, i: (bi, i, 0)),
                  pl.BlockSpec((1, d), lambda bi, i: (0, 0)),
                  pl.BlockSpec(ws.shape, lambda bi, i: (0, 0, 0)),
                  pl.BlockSpec(bs_t.shape, lambda bi, i: (0, 0)),
                  pl.BlockSpec(wp.shape, lambda bi, i: (0, 0, 0)),
                  pl.BlockSpec((1, d), lambda bi, i: (0, 0))],
        out_specs=pl.BlockSpec((None, tm, d), lambda bi, i: (bi, i, 0)),
        scratch_shapes=[pltpu.VMEM((tm, d), F32)],
        compiler_params=_cparams(("parallel", "arbitrary"), 56),
        name="mixer_merge",
    )(z, p, p, gates, ya, sgu_norm.reshape(1, d), ws, bs_t, wp, pool_scale.reshape(1, d))


def _matmul_res_kernel(a_ref, w_ref, x_ref, g_ref, o_ref):
    y = jnp.dot(a_ref[...], w_ref[...], preferred_element_type=F32)
    o_ref[...] = x_ref[...] + g_ref[...] * y


def matmul_res(a, w, x, gate):
    b, s, d = x.shape
    tm = min(512, s)
    return pl.pallas_call(
        _matmul_res_kernel,
        out_shape=jax.ShapeDtypeStruct((b, s, d), F32),
        grid=(b, s // tm),
        in_specs=[pl.BlockSpec((None, tm, d), lambda bi, i: (bi, i, 0)),
                  pl.BlockSpec((d, d), lambda bi, i: (0, 0)),
                  pl.BlockSpec((None, tm, d), lambda bi, i: (bi, i, 0)),
                  pl.BlockSpec((None, 1, d), lambda bi, i: (bi, 0, 0))],
        out_specs=pl.BlockSpec((None, tm, d), lambda bi, i: (bi, i, 0)),
        compiler_params=_cparams(("parallel", "parallel"), 56),
        name="matmul_res",
    )(a, w, x, gate)


def _pack_bf16_pair(a, b):
    pa = lax.bitcast_convert_type(a.astype(BF16).astype(F32), U32)
    pb = lax.bitcast_convert_type(b.astype(BF16).astype(F32), U32)
    return pa | (pb >> 16)


def _unpack_bf16_pair(w):
    hi = lax.bitcast_convert_type(w & jnp.uint32(0xFFFF0000), F32)
    lo = lax.bitcast_convert_type(w << 16, F32)
    return hi, lo


def _store_row_tiles(ref, rows):
    n, w = rows.shape
    for s in range(w // LANES):
        ref[pl.ds(s, n, stride=ROW_TILE), :] = rows[:, s * LANES:(s + 1) * LANES]


def _load_row_tiles(ref):
    n = ref.shape[0] // ROW_TILE
    return jnp.concatenate([ref[pl.ds(s, n, stride=ROW_TILE), :] for s in range(ROW_TILE)], axis=1)


def _router_kernel(x_ref, g_ref, sc_ref, sh_ref, wr_ref, br_ref,
                   hp_ref, e_ref, w_ref, r_ref, cnt_ref, carry_ref):
    tm, d = x_ref.shape
    half = d // 2

    @pl.when((pl.program_id(0) == 0) & (pl.program_id(1) == 0))
    def _():
        carry_ref[...] = jnp.zeros_like(carry_ref)

    h = _modulated_rmsnorm(x_ref[...], g_ref[...], sc_ref[...], sh_ref[...])
    _store_row_tiles(hp_ref, _pack_bf16_pair(h[:, :half], h[:, half:]))

    logits = lax.dot_general(wr_ref[...], h, (((1,), (1,)), ((), ())),
                             preferred_element_type=F32, precision=lax.Precision.HIGHEST) + br_ref[...]
    eio = lax.broadcasted_iota(I32, logits.shape, 0).astype(F32)
    onehot = jnp.zeros(logits.shape, F32)
    vals, sels = [], []
    rem = logits
    for k in range(TOP_K):
        m = jnp.max(rem, axis=0, keepdims=True)
        idx = jnp.min(jnp.where(rem == m, eio, float(N_EXPERTS)), axis=0, keepdims=True)
        sel = eio == idx
        rem = jnp.where(sel, -jnp.inf, rem)
        onehot = onehot + jnp.where(sel, 1.0, 0.0)
        vals.append(m)
        sels.append(sel)
        e_ref[k:k + 1, :] = idx.astype(I32)
    ex = [jnp.exp(v - vals[0]) for v in vals]
    den = ex[0] + ex[1] + ex[2] + ex[3]
    for k in range(TOP_K):
        w_ref[k:k + 1, :] = ex[k] / den

    before = lax.broadcasted_iota(I32, (tm, tm), 0) < lax.broadcasted_iota(I32, (tm, tm), 1)
    cum = jnp.dot(onehot.astype(BF16), jnp.where(before, 1.0, 0.0).astype(BF16),
                  preferred_element_type=F32) + carry_ref[...]
    for k in range(TOP_K):
        r_ref[k:k + 1, :] = jnp.sum(jnp.where(sels[k], cum, 0.0), axis=0, keepdims=True).astype(I32)
    carry_ref[...] = carry_ref[...] + jnp.sum(onehot, axis=1, keepdims=True)
    cnt_ref[...] = jnp.broadcast_to(carry_ref[...], cnt_ref.shape).astype(I32)


def router(x, gain, scale, shift, wr_t, b_router):
    b, s, d = x.shape
    tm = min(512, s)
    nt = s // tm
    t = b * s
    tok = lambda bi, i: (0, bi * nt + i)
    return pl.pallas_call(
        _router_kernel,
        out_shape=(jax.ShapeDtypeStruct((b, s * ROW_TILE, LANES), U32),
                   jax.ShapeDtypeStruct((TOP_K, t), I32),
                   jax.ShapeDtypeStruct((TOP_K, t), F32),
                   jax.ShapeDtypeStruct((TOP_K, t), I32),
                   jax.ShapeDtypeStruct((N_EXPERTS, LANES), I32)),
        grid=(b, nt),
        in_specs=[pl.BlockSpec((None, tm, d), lambda bi, i: (bi, i, 0)),
                  pl.BlockSpec((1, d), lambda bi, i: (0, 0)),
                  pl.BlockSpec((None, 1, d), lambda bi, i: (bi, 0, 0)),
                  pl.BlockSpec((None, 1, d), lambda bi, i: (bi, 0, 0)),
                  pl.BlockSpec((N_EXPERTS, d), lambda bi, i: (0, 0)),
                  pl.BlockSpec((N_EXPERTS, 1), lambda bi, i: (0, 0))],
        out_specs=(pl.BlockSpec((None, tm * ROW_TILE, LANES), lambda bi, i: (bi, i, 0)),
                   pl.BlockSpec((TOP_K, tm), tok),
                   pl.BlockSpec((TOP_K, tm), tok),
                   pl.BlockSpec((TOP_K, tm), tok),
                   pl.BlockSpec((N_EXPERTS, LANES), lambda bi, i: (0, 0))),
        scratch_shapes=[pltpu.VMEM((N_EXPERTS, 1), F32)],
        compiler_params=_cparams(("arbitrary", "arbitrary"), 40),
        name="router",
    )(x, gain.reshape(1, d), scale, shift, wr_t, b_router.reshape(N_EXPERTS, 1))


def _dest_kernel(ps_ref, e_ref, r_ref, o_ref):
    e = e_ref[...]
    base = jnp.zeros(e.shape, I32)
    for j in range(N_EXPERTS):
        base = jnp.where(e == j, ps_ref[j], base)
    o_ref[...] = base + r_ref[...]


def dest_rows(pad_start, e_t, r_t):
    return pl.pallas_call(
        _dest_kernel,
        out_shape=jax.ShapeDtypeStruct(e_t.shape, I32),
        grid_spec=pltpu.PrefetchScalarGridSpec(
            num_scalar_prefetch=1, grid=(1,),
            in_specs=[pl.BlockSpec(e_t.shape, lambda i, ps: (0, 0)),
                      pl.BlockSpec(e_t.shape, lambda i, ps: (0, 0))],
            out_specs=pl.BlockSpec(e_t.shape, lambda i, ps: (0, 0))),
        compiler_params=_cparams(("arbitrary",), 16),
        name="dest",
    )(pad_start, e_t, r_t)


def _experts_kernel(be_ref, src_first_ref, src_second_ref, src_ahead_ref, slot_prev_ref, slot_last_ref, hp_ref,
                    wg_ref, bg_ref, wu_ref, bu_ref, wd_ref, bd_ref,
                    yt_ref, xbuf, ybuf, gsem, ssem, *, spare_base):
    del be_ref
    i = pl.program_id(0)
    last = pl.num_programs(0) - 1
    cur = lax.rem(i, 2)
    nxt = 1 - cur
    xcur = lax.rem(i, 3)
    xahead = lax.rem(i + 2, 3)
    rows = ybuf.shape[1] // ROW_TILE

    def tile(ref, offset):
        return ref.at[pl.ds(pl.multiple_of(offset, ROW_TILE), ROW_TILE)]

    def gather_copy(src, r, buf):
        return pltpu.make_async_copy(tile(hp_ref, src), tile(xbuf.at[buf], r * ROW_TILE), gsem.at[buf])

    def scatter_copy(dst, r, buf):
        return pltpu.make_async_copy(tile(ybuf.at[buf], r * ROW_TILE), tile(yt_ref, dst), ssem.at[buf])

    def wait_gathers(buf):
        for r in range(rows):
            gather_copy(0, r, buf).wait()

    def wait_scatters(buf):
        for r in range(rows):
            scatter_copy(0, r, buf).wait()

    @pl.when(i == 0)
    def _():
        for r in range(rows):
            gather_copy(src_first_ref[r], r, 0).start()
            gather_copy(src_second_ref[r], r, 1).start()
        ybuf[1] = jnp.zeros(ybuf.shape[1:], ybuf.dtype)

    wait_gathers(xcur)

    @pl.when(i >= 1)
    def _():
        wait_scatters(cur)

    hi, lo = _unpack_bf16_pair(_load_row_tiles(xbuf.at[xcur]))
    x = jnp.concatenate([hi.astype(BF16), lo.astype(BF16)], axis=1)
    for r in range(rows):
        gather_copy(src_ahead_ref[r], r, xahead).start()
        dst = jnp.where(i >= 1, slot_prev_ref[r], spare_base + r * ROW_TILE)
        scatter_copy(dst, r, nxt).start()

    gl = jnp.dot(x, wg_ref[...], preferred_element_type=F32) + bg_ref[...]
    li = jnp.dot(x, wu_ref[...], preferred_element_type=F32) + bu_ref[...]
    gl = jnp.minimum(gl, SWIGLU_LIMIT)
    li = jnp.clip(li, -SWIGLU_LIMIT, SWIGLU_LIMIT)
    act = gl * jax.nn.sigmoid(SWIGLU_ALPHA * gl) * (li + 1.0)
    y = jnp.dot(act.astype(BF16), wd_ref[...], preferred_element_type=F32) + bd_ref[...]
    half = y.shape[1] // 2
    _store_row_tiles(ybuf.at[cur], _pack_bf16_pair(y[:, :half], y[:, half:]))

    @pl.when(i == last)
    def _():
        for r in range(rows):
            scatter_copy(slot_last_ref[r], r, cur).start()
        wait_scatters(nxt)
        wait_scatters(cur)
        wait_gathers(lax.rem(i + 1, 3))
        wait_gathers(xahead)


def experts(hp, slots, block_e, wg, bg, wu, bu, wd, bd):
    t = hp.shape[0] // ROW_TILE
    n_rows = slots.shape[0]
    n_blocks = n_rows // MOE_ROWS
    n_e, d, f = wg.shape
    spare_base = t * TOP_K + n_rows
    src = (slots % t) * ROW_TILE
    slots = slots * ROW_TILE
    kern = functools.partial(_experts_kernel, spare_base=spare_base * ROW_TILE)
    smem_block = lambda idx: pl.BlockSpec((MOE_ROWS,), idx, memory_space=pltpu.SMEM)
    return pl.pallas_call(
        kern,
        out_shape=jax.ShapeDtypeStruct(((spare_base + MOE_ROWS) * ROW_TILE, LANES), U32),
        grid_spec=pltpu.PrefetchScalarGridSpec(
            num_scalar_prefetch=1, grid=(n_blocks,),
            in_specs=[smem_block(lambda i, be: (0,)),
                      smem_block(lambda i, be: (1,)),
                      smem_block(lambda i, be: (jnp.minimum(i + 2, n_blocks - 1),)),
                      smem_block(lambda i, be: (jnp.maximum(i - 1, 0),)),
                      smem_block(lambda i, be: (n_blocks - 1,)),
                      pl.BlockSpec(memory_space=pl.ANY),
                      pl.BlockSpec((None, d, f), lambda i, be: (be[i], 0, 0)),
                      pl.BlockSpec((None, 1, f), lambda i, be: (be[i], 0, 0)),
                      pl.BlockSpec((None, d, f), lambda i, be: (be[i], 0, 0)),
                      pl.BlockSpec((None, 1, f), lambda i, be: (be[i], 0, 0)),
                      pl.BlockSpec((None, f, d), lambda i, be: (be[i], 0, 0)),
                      pl.BlockSpec((None, 1, d), lambda i, be: (be[i], 0, 0))],
            out_specs=pl.BlockSpec(memory_space=pl.ANY),
            scratch_shapes=[pltpu.VMEM((3, MOE_ROWS * ROW_TILE, LANES), U32),
                            pltpu.VMEM((2, MOE_ROWS * ROW_TILE, LANES), U32),
                            pltpu.SemaphoreType.DMA((3,)), pltpu.SemaphoreType.DMA((2,))]),
        compiler_params=_cparams(("arbitrary",), 56),
        name="experts",
    )(block_e, src, src, src, slots, slots, hp, wg, bg.reshape(n_e, 1, f), wu, bu.reshape(n_e, 1, f), wd,
      bd.reshape(n_e, 1, d))


def _combine_kernel(w_ref, y0_ref, y1_ref, y2_ref, y3_ref, x_ref, g_ref, fn_ref, o_ref, *, final):
    half = x_ref.shape[1] // 2
    acc_hi = None
    for k, y_ref in enumerate((y0_ref, y1_ref, y2_ref, y3_ref)):
        hi, lo = _unpack_bf16_pair(_load_row_tiles(y_ref))
        wk = w_ref[:, k:k + 1]
        if acc_hi is None:
            acc_hi, acc_lo = wk * hi, wk * lo
        else:
            acc_hi, acc_lo = acc_hi + wk * hi, acc_lo + wk * lo
    out = x_ref[...] + g_ref[...] * jnp.concatenate([acc_hi, acc_lo], axis=1)
    if final:
        out = _rms(out, fn_ref[...])
    o_ref[...] = out


def combine(w_tok, yt, x, gate, final_norm, final):
    b, s, d = x.shape
    tm = min(512, s)
    nt = s // tm
    tiles = b * nt
    y_spec = lambda k: pl.BlockSpec((tm * ROW_TILE, LANES), lambda bi, i: (k * tiles + bi * nt + i, 0))
    return pl.pallas_call(
        functools.partial(_combine_kernel, final=final),
        out_shape=jax.ShapeDtypeStruct((b, s, d), F32),
        grid=(b, nt),
        in_specs=[pl.BlockSpec((tm, TOP_K), lambda bi, i: (bi * nt + i, 0)),
                  y_spec(0), y_spec(1), y_spec(2), y_spec(3),
                  pl.BlockSpec((None, tm, d), lambda bi, i: (bi, i, 0)),
                  pl.BlockSpec((None, 1, d), lambda bi, i: (bi, 0, 0)),
                  pl.BlockSpec((1, d), lambda bi, i: (0, 0))],
        out_specs=pl.BlockSpec((None, tm, d), lambda bi, i: (bi, i, 0)),
        compiler_params=_cparams(("parallel", "parallel"), 48),
        name="combine",
    )(w_tok, yt, yt, yt, yt, x, gate, final_norm.reshape(1, d))


def _swap_halves(w):
    half = w.shape[-1] // 2
    return jnp.concatenate([w[..., half:], w[..., :half]], axis=-1)


def _rope_tables(s):
    pos = jnp.arange(s, dtype=F32)
    inv_freq = ROPE_THETA ** (-jnp.arange(0, QK_ROPE, 2, dtype=F32) / QK_ROPE)
    ang = pos[:, None] * inv_freq[None, :]
    cos, sin = jnp.cos(ang), jnp.sin(ang)
    zero = jnp.zeros((s, LANES - QK_ROPE), F32)
    return (jnp.concatenate([cos, cos, zero], axis=1), jnp.concatenate([-sin, sin, zero], axis=1))


def kernel(x, c, attn_norm, ffn_norm, w_ada, b_ada, w_in, q_norm, kv_norm, w_uq, w_ukv, sgu_norm, w_s, b_s,
           w_pool, pool_scale, w_o, w_router, b_router, w_gate, b_gate, w_up, b_up, w_down, b_down, final_norm):
    b, s, d = x.shape
    n_layers = w_ada.shape[0]
    t = b * s
    mla_in = Q_LORA + KV_LORA + QK_ROPE
    mods = adaln(c, w_ada, b_ada)
    cc, ss = _rope_tables(s)
    n_blocks = t * TOP_K // MOE_ROWS + N_EXPERTS
    n_rows = n_blocks * MOE_ROWS

    for l in range(n_layers):
        sh1, sc1, g1, sh2, sc2, g2 = [mods[l, :b, i * d:(i + 1) * d].reshape(b, 1, d) for i in range(6)]

        wl = w_in[l]
        k_rope_cols = wl[:, Q_LORA + KV_LORA:mla_in]
        w_lat = jnp.concatenate([wl[:, :mla_in], _swap_halves(k_rope_cols)], axis=1).astype(BF16)
        w_z = wl[:, mla_in:mla_in + 2 * d].astype(BF16)
        w_p = wl[:, mla_in + 2 * d:mla_in + 3 * d].astype(BF16)
        w_g = wl[:, mla_in + 3 * d:].astype(BF16)
        h = mod_norm(x, attn_norm[l], sc1, sh1)
        lat = proj(h, w_lat, w_lat.shape[1])
        z = proj(h, w_z, 1024, "gelu", BF16)
        p = proj(h, w_p, 1024)
        gates = proj(h, w_g, 1024, "sigmoid", BF16)

        wq = w_uq[l]
        wq = jnp.concatenate([wq, _swap_halves(wq[..., QK_NOPE:])], axis=-1)
        wq = wq.transpose(1, 0, 2).astype(BF16)
        wkv = w_ukv[l].transpose(1, 0, 2).astype(BF16)
        q, k, v = mla_prep(lat, q_norm[l], kv_norm[l], cc, ss, wq, wkv)
        ya = flash(q, k, v)

        merged = mixer_merge(z, p, gates, ya, sgu_norm[l], w_s[l].astype(BF16), b_s[l].T,
                             w_pool[l].astype(BF16), pool_scale[l])
        x = matmul_res(merged, w_o[l].astype(BF16), x, g1)

        hp, e_t, w_t, r_t, cnt = router(x, ffn_norm[l], sc2, sh2, w_router[l].T, b_router[l])
        counts = cnt[:, 0]
        padded = ((counts + MOE_ROWS - 1) // MOE_ROWS) * MOE_ROWS
        pad_end = jnp.cumsum(padded)
        pad_start = pad_end - padded
        block_start = jnp.arange(n_blocks, dtype=I32) * MOE_ROWS
        block_e = jnp.minimum(jnp.sum((pad_end[None, :] <= block_start[:, None]).astype(I32), axis=1),
                              N_EXPERTS - 1)
        dest_flat = dest_rows(pad_start.astype(I32), e_t, r_t).reshape(TOP_K * t)
        slots = (t * TOP_K + jnp.arange(n_rows, dtype=I32)).at[dest_flat].set(
            jnp.arange(t * TOP_K, dtype=I32), unique_indices=True, mode="promise_in_bounds")
        yt = experts(hp.reshape(t * ROW_TILE, LANES), slots, block_e, w_gate[l].astype(BF16), b_gate[l],
                     w_up[l].astype(BF16), b_up[l], w_down[l].astype(BF16), b_down[l])
        x = combine(w_t.T, yt, x, g2, final_norm, final=(l == n_layers - 1))
    return x
```
